```python
import math
import jax, jax.numpy as jnp
from jax import lax
import numpy as np

D_MODEL = 1024
BATCH = 8
SEQ = 4096
DEPTH = 2
DEC_BATCH = 32
DEC_SEQ = 8
PAST_LEN = 16384
PAGE_SIZE = 128

HEAD_DIM = 64
POOL_WIDTH = 256
POOL_WINDOWS = (2, 4, 8, 16)
POOL_GROUPS = len(POOL_WINDOWS)
POOL_GROUP_DIM = POOL_WIDTH // POOL_GROUPS
POOL_HIST = max(POOL_WINDOWS) - 1
SG_WIDTH = 256
SG_HEADS = SG_WIDTH // HEAD_DIM
SG_CHUNK = 128
MOBA_WIDTH = 512
MOBA_HEADS = MOBA_WIDTH // HEAD_DIM
MOBA_BLOCK = 256
MOBA_TOPK = 3
MOBA_Q_BLOCK = 32
MIX_WIDTH = POOL_WIDTH + SG_WIDTH + MOBA_WIDTH
IN_WIDTH = POOL_WIDTH + 2 * SG_WIDTH + 3 * MOBA_WIDTH
N_MEM = 256
X_HEADS = 4
X_HEAD_DIM = D_MODEL // X_HEADS
D_FF = -(-8 * D_MODEL // (3 * 256)) * 256
DEEPNORM_ALPHA = (2 * DEPTH) ** 0.25
DEEPNORM_BETA = (8 * DEPTH) ** -0.25
LN_EPS = 1e-5

kernel_name = "hymba_pool_gmlp_moba_deepnorm_step"


def layer_norm(x, g, b):
    xf = x.astype(jnp.float32)
    mu = jnp.mean(xf, axis=-1, keepdims=True)
    var = jnp.mean(jnp.square(xf - mu), axis=-1, keepdims=True)
    return ((xf - mu) * lax.rsqrt(var + LN_EPS) * g + b).astype(x.dtype)


def pool_mixer(a, hist, pos0, pool_w, pool_scale):
    B, S, _ = a.shape
    ext = jnp.concatenate([hist, a], axis=1).astype(jnp.float32)
    csum = jnp.concatenate([jnp.zeros((B, 1, POOL_WIDTH), jnp.float32),
                            jnp.cumsum(ext, axis=1)], axis=1)
    end = csum[:, POOL_HIST + 1:]
    pos = pos0 + jnp.arange(S)
    means = []
    for g, w in enumerate(POOL_WINDOWS):
        sl = slice(g * POOL_GROUP_DIM, (g + 1) * POOL_GROUP_DIM)
        start = csum[:, POOL_HIST + 1 - w: POOL_HIST + 1 - w + S, sl]
        cnt = jnp.minimum(w, pos + 1).astype(jnp.float32)[None, :, None]
        means.append((end[..., sl] - start) / cnt)
    pooled = (jnp.concatenate(means, axis=-1) - ext[:, POOL_HIST:]).reshape(B, S, POOL_GROUPS, POOL_GROUP_DIM)
    y = jnp.einsum('bsgc,gcd->bsgd', pooled, pool_w.astype(jnp.float32)).reshape(B, S, POOL_WIDTH) * pool_scale
    return y.astype(a.dtype), ext[:, -POOL_HIST:].astype(a.dtype)


def spatial_gate(zb, n_g, n_b, sg_w, sg_b):
    B, S, _ = zb.shape
    z = jax.nn.gelu(zb)
    u, v = z[..., :SG_WIDTH], z[..., SG_WIDTH:]
    v = layer_norm(v, n_g, n_b)
    L = min(S, SG_CHUNK)
    n_c = S // L
    causal = jnp.tril(jnp.ones((L, L), bool))
    w = jnp.where(causal[None], sg_w[:, :L, :L], 0)
    vc = v.reshape(B, n_c, L, SG_HEADS, HEAD_DIM)
    s = jnp.einsum('hij,bcjhd->bcihd', w, vc) + sg_b[:, :L].T[None, None, :, :, None]
    return u * s.reshape(B, S, SG_WIDTH), v


def moba_attention(q, k, v, q_offset):
    B, Sq, H, dh = q.shape
    Lk = k.shape[1]
    nb = -(-Lk // MOBA_BLOCK)
    pad = nb * MOBA_BLOCK - Lk
    kb = jnp.pad(k, ((0, 0), (0, pad), (0, 0), (0, 0))).reshape(B, nb, MOBA_BLOCK, H, dh)
    vb = jnp.pad(v, ((0, 0), (0, pad), (0, 0), (0, 0))).reshape(B, nb, MOBA_BLOCK, H, dh)
    kmean = jnp.mean(kb.astype(jnp.float32), axis=2)
    n_sel = min(MOBA_TOPK, nb)
    qc = MOBA_Q_BLOCK if Sq % MOBA_Q_BLOCK == 0 else Sq
    n_chunks = Sq // qc
    q_chunks = jnp.moveaxis(q.reshape(B, n_chunks, qc, H, dh), 1, 0)
    b_ix = jnp.arange(B)[:, None, None, None]
    h_ix = jnp.arange(H)[None, :, None, None]
    blk_ids = jnp.arange(nb)
    in_blk = jnp.arange(MOBA_BLOCK)
    scale = dh ** -0.5

    def one_chunk(args):
        qch, c = args
        qpos = q_offset + c * qc + jnp.arange(qc)
        qblk = qpos // MOBA_BLOCK
        gate = jnp.einsum('bqhd,bnhd->bhqn', qch.astype(jnp.float32), kmean)
        gate = jnp.where(blk_ids[None, :] < qblk[:, None], gate, -jnp.inf)
        _, sel = lax.top_k(gate, n_sel)
        own = jnp.broadcast_to(qblk[None, None, :, None], (B, H, qc, 1)).astype(sel.dtype)
        idx = jnp.concatenate([sel, own], axis=-1)
        slot_ok = jnp.concatenate([jnp.arange(n_sel)[None, :] < qblk[:, None],
                                   jnp.ones((qc, 1), bool)], axis=-1)
        kg = kb[b_ix, idx, :, h_ix]
        vg = vb[b_ix, idx, :, h_ix]
        kpos = idx[..., None] * MOBA_BLOCK + in_blk
        mask = slot_ok[None, None, :, :, None] & (kpos <= qpos[None, None, :, None, None])
        s = jnp.einsum('bqhd,bhqjkd->bhqjk', qch, kg, preferred_element_type=jnp.float32) * scale
        s = jnp.where(mask, s, -jnp.inf).reshape(B, H, qc, -1)
        p = jax.nn.softmax(s, axis=-1).astype(v.dtype)
        return jnp.einsum('bhqm,bhqmd->bqhd', p, vg.reshape(B, H, qc, -1, dh))

    out = lax.map(one_chunk, (q_chunks, jnp.arange(n_chunks)))
    return jnp.moveaxis(out, 0, 1).reshape(B, Sq, H * dh)


def memory_attention(x, mem_k, mem_v, w_q, w_o):
    B, S, _ = x.shape
    q = (x @ w_q).reshape(B, S, X_HEADS, X_HEAD_DIM)
    s = jnp.einsum('bshd,bmhd->bhsm', q, mem_k, preferred_element_type=jnp.float32) * (X_HEAD_DIM ** -0.5)
    p = jax.nn.softmax(s, axis=-1).astype(x.dtype)
    o = jnp.einsum('bhsm,bmhd->bshd', p, mem_v)
    return o.reshape(B, S, D_MODEL) @ w_o


def trunk_layer(x, pool_hist, past_k, past_v, mem_k, mem_v, w):
    (w_in, pool_w, pool_scale, sg_norm_g, sg_norm_b, sg_w, sg_b, w_out, ln1_g, ln1_b,
     xq_w, xo_w, ln2_g, ln2_b, ffn_gate, ffn_up, ffn_down, ln3_g, ln3_b) = w
    B, S, _ = x.shape
    pos0 = 0 if past_k is None else past_k.shape[1]
    h = x @ w_in
    a = h[..., :POOL_WIDTH]
    zb = h[..., POOL_WIDTH:POOL_WIDTH + 2 * SG_WIDTH]
    q, k, v = jnp.split(h[..., POOL_WIDTH + 2 * SG_WIDTH:], 3, axis=-1)
    q = q.reshape(B, S, MOBA_HEADS, HEAD_DIM)
    k = k.reshape(B, S, MOBA_HEADS, HEAD_DIM)
    v = v.reshape(B, S, MOBA_HEADS, HEAD_DIM)
    y_pool, new_hist = pool_mixer(a, pool_hist, pos0, pool_w, pool_scale)
    y_sg, sg_v = spatial_gate(zb, sg_norm_g, sg_norm_b, sg_w, sg_b)
    if past_k is None:
        k_all, v_all = k, v
    else:
        k_all = jnp.concatenate([past_k, k], axis=1)
        v_all = jnp.concatenate([past_v, v], axis=1)
    y_moba = moba_attention(q, k_all, v_all, pos0)
    mix = jnp.concatenate([y_pool, y_sg, y_moba], axis=-1) @ w_out
    x = layer_norm(DEEPNORM_ALPHA * x + mix, ln1_g, ln1_b)
    x = layer_norm(DEEPNORM_ALPHA * x + memory_attention(x, mem_k, mem_v, xq_w, xo_w), ln2_g, ln2_b)
    ffn = (jax.nn.silu(x @ ffn_gate) * (x @ ffn_up)) @ ffn_down
    x = layer_norm(DEEPNORM_ALPHA * x + ffn, ln3_g, ln3_b)
    return x, new_hist, k, v, sg_v


def setup_inputs(seed: int = 0) -> dict:
    key = jax.random.key(seed)
    ks = iter(jax.random.split(key, 40))

    def nrm(shape, scale=1.0):
        return jax.random.normal(next(ks), shape, jnp.float32) * scale

    n_pages = PAST_LEN // PAGE_SIZE
    n_used = DEC_BATCH * n_pages
    n_phys = n_used + n_used // 4
    page_table = jax.random.permutation(next(ks), n_phys)[:n_used].reshape(DEC_BATCH, n_pages).astype(jnp.int32)
    L = DEPTH
    return {
        "x_prompt": nrm((BATCH, SEQ, D_MODEL)),
        "x_sample": nrm((DEC_BATCH, DEC_SEQ, D_MODEL)),
        "mem_prompt": nrm((BATCH, N_MEM, D_MODEL)),
        "state_pool": nrm((L, DEC_BATCH, POOL_HIST, POOL_WIDTH)),
        "cache_k": nrm((L, n_phys, PAGE_SIZE, MOBA_HEADS, HEAD_DIM)),
        "cache_v": nrm((L, n_phys, PAGE_SIZE, MOBA_HEADS, HEAD_DIM)),
        "cache_mem_k": nrm((L, DEC_BATCH, N_MEM, X_HEADS, X_HEAD_DIM)),
        "cache_mem_v": nrm((L, DEC_BATCH, N_MEM, X_HEADS, X_HEAD_DIM)),
        "page_table": page_table,
        "w_in": nrm((L, D_MODEL, IN_WIDTH), D_MODEL ** -0.5),
        "pool_w": nrm((L, POOL_GROUPS, POOL_GROUP_DIM, POOL_GROUP_DIM), POOL_GROUP_DIM ** -0.5),
        "pool_scale": 1.0 + nrm((L, POOL_WIDTH), 0.05),
        "sg_norm_g": 1.0 + nrm((L, SG_WIDTH), 0.02),
        "sg_norm_b": nrm((L, SG_WIDTH), 0.02),
        "sg_w": nrm((L, SG_HEADS, SG_CHUNK, SG_CHUNK), SG_CHUNK ** -0.5),
        "sg_b": 1.0 + nrm((L, SG_HEADS, SG_CHUNK), 0.02),
        "w_out": nrm((L, MIX_WIDTH, D_MODEL), MIX_WIDTH ** -0.5 * DEEPNORM_BETA),
        "ln1_g": 1.0 + nrm((L, D_MODEL), 0.02),
        "ln1_b": nrm((L, D_MODEL), 0.02),
        "xq_w": nrm((L, D_MODEL, D_MODEL), D_MODEL ** -0.5),
        "xk_w": nrm((L, D_MODEL, D_MODEL), D_MODEL ** -0.5),
        "xv_w": nrm((L, D_MODEL, D_MODEL), D_MODEL ** -0.5),
        "xo_w": nrm((L, D_MODEL, D_MODEL), D_MODEL ** -0.5 * DEEPNORM_BETA),
        "ln2_g": 1.0 + nrm((L, D_MODEL), 0.02),
        "ln2_b": nrm((L, D_MODEL), 0.02),
        "ffn_gate": nrm((L, D_MODEL, D_FF), D_MODEL ** -0.5),
        "ffn_up": nrm((L, D_MODEL, D_FF), D_MODEL ** -0.5),
        "ffn_down": nrm((L, D_FF, D_MODEL), D_FF ** -0.5 * DEEPNORM_BETA),
        "ln3_g": 1.0 + nrm((L, D_MODEL), 0.02),
        "ln3_b": nrm((L, D_MODEL), 0.02),
    }


def reference(x_prompt, x_sample, mem_prompt, state_pool, cache_k, cache_v, cache_mem_k, cache_mem_v,
              page_table, w_in, pool_w, pool_scale, sg_norm_g, sg_norm_b, sg_w, sg_b, w_out, ln1_g, ln1_b,
              xq_w, xk_w, xv_w, xo_w, ln2_g, ln2_b, ffn_gate, ffn_up, ffn_down, ln3_g, ln3_b):
    B = x_prompt.shape[0]
    DB = x_sample.shape[0]
    n_pages = page_table.shape[1]
    past_len = n_pages * PAGE_SIZE
    xp, xs = x_prompt, x_sample
    pool_p, pool_s, kp_l, vp_l, ks_l, vs_l, sgv_l, mk_l, mv_l = [], [], [], [], [], [], [], [], []
    for l in range(DEPTH):
        w = (w_in[l], pool_w[l], pool_scale[l], sg_norm_g[l], sg_norm_b[l], sg_w[l], sg_b[l], w_out[l],
             ln1_g[l], ln1_b[l], xq_w[l], xo_w[l], ln2_g[l], ln2_b[l], ffn_gate[l], ffn_up[l], ffn_down[l],
             ln3_g[l], ln3_b[l])
        mem_k = (mem_prompt @ xk_w[l]).reshape(B, N_MEM, X_HEADS, X_HEAD_DIM)
        mem_v = (mem_prompt @ xv_w[l]).reshape(B, N_MEM, X_HEADS, X_HEAD_DIM)
        hist0 = jnp.zeros((B, POOL_HIST, POOL_WIDTH), xp.dtype)
        xp, hist_p, k_p, v_p, _ = trunk_layer(xp, hist0, None, None, mem_k, mem_v, w)
        k_past = cache_k[l, page_table].reshape(DB, past_len, MOBA_HEADS, HEAD_DIM)
        v_past = cache_v[l, page_table].reshape(DB, past_len, MOBA_HEADS, HEAD_DIM)
        xs, hist_s, k_s, v_s, sgv_s = trunk_layer(xs, state_pool[l], k_past, v_past,
                                                  cache_mem_k[l], cache_mem_v[l], w)
        pool_p.append(hist_p); pool_s.append(hist_s)
        kp_l.append(k_p); vp_l.append(v_p); ks_l.append(k_s); vs_l.append(v_s)
        sgv_l.append(sgv_s); mk_l.append(mem_k); mv_l.append(mem_v)
    y_prompt = xp
    y_sample = xs
    new_pool_prompt = jnp.stack(pool_p)
    new_pool_sample = jnp.stack(pool_s)
    new_k_prompt = jnp.stack(kp_l)
    new_v_prompt = jnp.stack(vp_l)
    new_k_sample = jnp.stack(ks_l)
    new_v_sample = jnp.stack(vs_l)
    new_sg_v_sample = jnp.stack(sgv_l)
    new_mem_k_prompt = jnp.stack(mk_l)
    new_mem_v_prompt = jnp.stack(mv_l)
    return (y_prompt, y_sample, new_pool_prompt, new_pool_sample, new_k_prompt, new_v_prompt,
            new_k_sample, new_v_sample, new_sg_v_sample, new_mem_k_prompt, new_mem_v_prompt)
```

```python
import functools

import jax
import jax.numpy as jnp
from jax import lax
from jax.experimental import pallas as pl
from jax.experimental.pallas import tpu as pltpu

F32 = jnp.float32
BF16 = jnp.bfloat16

HEAD_DIM = 64
POOL_WIDTH = 256
POOL_WINDOWS = (2, 4, 8, 16)
POOL_GROUP_DIM = POOL_WIDTH // len(POOL_WINDOWS)
POOL_HIST = max(POOL_WINDOWS) - 1
HIST_ROWS = POOL_HIST + 1
SG_WIDTH = 256
SG_HEADS = SG_WIDTH // HEAD_DIM
SG_CHUNK = 128
MOBA_WIDTH = 512
MOBA_HEADS = MOBA_WIDTH // HEAD_DIM
MOBA_BLOCK = 256
MOBA_TOPK = 3
PAGE_SIZE = 128
X_HEADS = 4
LN_EPS = 1e-5
MASKED = -1e30

LANES = 128
VMEM_LIMIT_BYTES = 56 * 1024 * 1024
ROW_TILE = 512


def _dot(a, b):
    return jnp.dot(a, b, preferred_element_type=F32)


def _dot_nt(a, b, precision=None):
    return lax.dot_general(a, b, (((1,), (1,)), ((), ())), preferred_element_type=F32, precision=precision)


def _layer_norm(y, g, b):
    mu = jnp.mean(y, axis=-1, keepdims=True)
    d = y - mu
    var = jnp.mean(d * d, axis=-1, keepdims=True)
    return d * lax.rsqrt(var + LN_EPS) * g + b


def _params(sem):
    return pltpu.CompilerParams(dimension_semantics=sem, vmem_limit_bytes=VMEM_LIMIT_BYTES)


def _resident(shape):
    nd = len(shape)
    return pl.BlockSpec(shape, lambda *_: (0,) * nd, pipeline_mode=pl.Buffered(1))


def _window_sums(ext):
    s2 = ext + pltpu.roll(ext, 1, 0)
    s4 = s2 + pltpu.roll(s2, 2, 0)
    s8 = s4 + pltpu.roll(s4, 4, 0)
    s16 = s8 + pltpu.roll(s8, 8, 0)
    lane = lax.broadcasted_iota(jnp.int32, ext.shape, 1)
    g = POOL_GROUP_DIM
    return jnp.where(lane < g, s2, jnp.where(lane < 2 * g, s4, jnp.where(lane < 3 * g, s8, s16)))


def _pool_out(sums, a, pos, pw_bd, scale):
    lane = lax.broadcasted_iota(jnp.int32, a.shape, 1)
    g = POOL_GROUP_DIM
    win = jnp.where(lane < g, 2, jnp.where(lane < 2 * g, 4, jnp.where(lane < 3 * g, 8, 16)))
    cnt = jnp.minimum(win, pos + 1).astype(F32)
    pooled = sums / cnt - a
    return _dot(pooled.astype(BF16), pw_bd) * scale


def _spatial_gate(zb, n_g, n_b, sgw_ref, bias, chunk):
    z = jax.nn.gelu(zb, approximate=True)
    u = z[:, :SG_WIDTH]
    v = _layer_norm(z[:, SG_WIDTH:], n_g, n_b)
    vb = v.astype(BF16)
    lane = lax.broadcasted_iota(jnp.int32, (chunk, LANES), 1)
    rows = []
    for c in range(zb.shape[0] // chunk):
        vc = vb[c * chunk:(c + 1) * chunk]
        cols = []
        for pair in range(SG_HEADS // 2):
            vp = vc[:, pair * LANES:(pair + 1) * LANES]
            s0 = _dot(sgw_ref[2 * pair], vp)
            s1 = _dot(sgw_ref[2 * pair + 1], vp)
            cols.append(jnp.where(lane < HEAD_DIM, s0, s1))
        rows.append(jnp.concatenate(cols, axis=1) + bias)
    s = rows[0] if len(rows) == 1 else jnp.concatenate(rows, axis=0)
    return u * s, v


def _project_qkv(xb, w_ref, q_ref, k_ref, v_ref, kb_ref, vb_ref):
    c0 = POOL_WIDTH + 2 * SG_WIDTH
    q = _dot(xb, w_ref[:, c0:c0 + MOBA_WIDTH]) * (HEAD_DIM ** -0.5)
    q_ref[...] = q.astype(q_ref.dtype)
    k = _dot(xb, w_ref[:, c0 + MOBA_WIDTH:c0 + 2 * MOBA_WIDTH])
    k_ref[...] = k
    v = _dot(xb, w_ref[:, c0 + 2 * MOBA_WIDTH:c0 + 3 * MOBA_WIDTH])
    v_ref[...] = v
    if kb_ref is not None:
        kb_ref[...] = k.astype(BF16)
        vb_ref[...] = v.astype(BF16)


def _in_prompt_kernel(x_ref, w_ref, pw_ref, ps_ref, ng_ref, nb_ref, sgw_ref, sgb_ref,
                      yab_ref, q_ref, k_ref, v_ref, kb_ref, vb_ref, hist_ref, ext_ref, *, tiles_per_seq):
    tm = x_ref.shape[0]
    t = pl.program_id(0) % tiles_per_seq

    @pl.when(t == 0)
    def _():
        ext_ref[0:HIST_ROWS, :] = jnp.zeros((HIST_ROWS, POOL_WIDTH), F32)

    xb = x_ref[...].astype(BF16)
    a = _dot(xb, w_ref[:, 0:POOL_WIDTH])
    ext_ref[HIST_ROWS:, :] = a
    sums = _window_sums(ext_ref[...])[HIST_ROWS:]
    pos = t * tm + lax.broadcasted_iota(jnp.int32, a.shape, 0)
    yab_ref[:, 0:POOL_WIDTH] = _pool_out(sums, a, pos, pw_ref[...], ps_ref[...]).astype(yab_ref.dtype)
    tail = ext_ref[tm:tm + HIST_ROWS, :]
    hist_ref[...] = tail
    ext_ref[0:HIST_ROWS, :] = tail

    zb = _dot(xb, w_ref[:, POOL_WIDTH:POOL_WIDTH + 2 * SG_WIDTH])
    y_sg, _ = _spatial_gate(zb, ng_ref[...], nb_ref[...], sgw_ref, sgb_ref[...], SG_CHUNK)
    yab_ref[:, POOL_WIDTH:] = y_sg.astype(yab_ref.dtype)
    _project_qkv(xb, w_ref, q_ref, k_ref, v_ref, kb_ref, vb_ref)


def _in_prompt_call(x, w_in, pw_bd, pscale, n_g, n_b, sgw, sgb, *, seq, tm):
    n, d = x.shape
    nb = n // seq
    tiles_per_seq = seq // tm
    row = lambda i: (i, 0)
    out_shape = (
        jax.ShapeDtypeStruct((n, POOL_WIDTH + SG_WIDTH), BF16),
        jax.ShapeDtypeStruct((n, MOBA_WIDTH), BF16),
        jax.ShapeDtypeStruct((n, MOBA_WIDTH), F32),
        jax.ShapeDtypeStruct((n, MOBA_WIDTH), F32),
        jax.ShapeDtypeStruct((n, MOBA_WIDTH), BF16),
        jax.ShapeDtypeStruct((n, MOBA_WIDTH), BF16),
        jax.ShapeDtypeStruct((nb, HIST_ROWS, POOL_WIDTH), F32),
    )
    out_specs = (
        pl.BlockSpec((tm, POOL_WIDTH + SG_WIDTH), row),
        pl.BlockSpec((tm, MOBA_WIDTH), row),
        pl.BlockSpec((tm, MOBA_WIDTH), row),
        pl.BlockSpec((tm, MOBA_WIDTH), row),
        pl.BlockSpec((tm, MOBA_WIDTH), row),
        pl.BlockSpec((tm, MOBA_WIDTH), row),
        pl.BlockSpec((None, HIST_ROWS, POOL_WIDTH), lambda i: (i // tiles_per_seq, 0, 0)),
    )
    in_specs = [pl.BlockSpec((tm, d), row), _resident(w_in.shape), _resident(pw_bd.shape),
                _resident(pscale.shape), _resident(n_g.shape), _resident(n_b.shape),
                _resident(sgw.shape), _resident(sgb.shape)]
    return pl.pallas_call(
        functools.partial(_in_prompt_kernel, tiles_per_seq=tiles_per_seq),
        grid=(n // tm,), in_specs=in_specs, out_specs=out_specs, out_shape=out_shape,
        scratch_shapes=[pltpu.VMEM((HIST_ROWS + tm, POOL_WIDTH), F32)],
        compiler_params=_params(("arbitrary",)), name="in_prompt",
    )(x, w_in, pw_bd, pscale, n_g, n_b, sgw, sgb)


def _in_sample_kernel(x_ref, hist_in_ref, w_ref, pw_ref, ps_ref, ng_ref, nb_ref, sgw_ref, sgb_ref,
                      yab_ref, q_ref, k_ref, v_ref, hist_ref, sgv_ref, *, dec_seq, pos0):
    n = x_ref.shape[0]
    nb = n // dec_seq
    seg = HIST_ROWS + dec_seq
    xb = x_ref[...].astype(BF16)
    a = _dot(xb, w_ref[:, 0:POOL_WIDTH])
    ext = jnp.concatenate([hist_in_ref[...], a.reshape(nb, dec_seq, POOL_WIDTH)], axis=1)
    sums = _window_sums(ext.reshape(nb * seg, POOL_WIDTH)).reshape(nb, seg, POOL_WIDTH)
    sums = sums[:, HIST_ROWS:, :].reshape(n, POOL_WIDTH)
    pos = pos0 + lax.broadcasted_iota(jnp.int32, a.shape, 0) % dec_seq
    yab_ref[:, 0:POOL_WIDTH] = _pool_out(sums, a, pos, pw_ref[...], ps_ref[...]).astype(yab_ref.dtype)
    hist_ref[...] = ext[:, seg - HIST_ROWS:, :]

    zb = _dot(xb, w_ref[:, POOL_WIDTH:POOL_WIDTH + 2 * SG_WIDTH])
    y_sg, v_n = _spatial_gate(zb, ng_ref[...], nb_ref[...], sgw_ref, sgb_ref[...], n)
    yab_ref[:, POOL_WIDTH:] = y_sg.astype(yab_ref.dtype)
    sgv_ref[...] = v_n
    _project_qkv(xb, w_ref, q_ref, k_ref, v_ref, None, None)


def _in_sample_call(x, hist16, w_in, pw_bd, pscale, n_g, n_b, sgw_bd, sgb_t, *, dec_seq, pos0):
    n, d = x.shape
    nb = n // dec_seq
    out_shape = (
        jax.ShapeDtypeStruct((n, POOL_WIDTH + SG_WIDTH), BF16),
        jax.ShapeDtypeStruct((n, MOBA_WIDTH), F32),
        jax.ShapeDtypeStruct((n, MOBA_WIDTH), F32),
        jax.ShapeDtypeStruct((n, MOBA_WIDTH), F32),
        jax.ShapeDtypeStruct((nb, HIST_ROWS, POOL_WIDTH), F32),
        jax.ShapeDtypeStruct((n, SG_WIDTH), F32),
    )
    args = (x, hist16, w_in, pw_bd, pscale, n_g, n_b, sgw_bd, sgb_t)
    return pl.pallas_call(
        functools.partial(_in_sample_kernel, dec_seq=dec_seq, pos0=pos0),
        grid=(1,), in_specs=[_resident(a.shape) for a in args],
        out_specs=tuple(_resident(s.shape) for s in out_shape), out_shape=out_shape,
        compiler_params=_params(("arbitrary",)), name="in_sample",
    )(*args)


def _block_rank(g, blk, n_valid, n_blocks, axis):
    rank = jnp.zeros(g.shape, jnp.int32)
    for m in range(n_blocks):
        gm = lax.slice_in_dim(g, m, m + 1, axis=axis)
        beats = (gm > g) | ((gm == g) & (m < blk))
        rank = rank + jnp.where(beats & (m < n_valid), 1, 0)
    return rank


def _moba_prompt_kernel(q_ref, k_ref, v_ref, o_ref, kaug_ref, kmean_ref, qaug_ref, m_ref, l_ref, acc_ref,
                        *, n_blocks):
    blk_rows = MOBA_BLOCK
    qi = pl.program_id(2)

    @pl.when(qi == 0)
    def _():
        lane = lax.broadcasted_iota(jnp.int32, (blk_rows, LANES), 1)
        kmean_ref[...] = jnp.zeros(kmean_ref.shape, F32)
        for n in range(n_blocks):
            kb = k_ref[n * blk_rows:(n + 1) * blk_rows, :]
            kaug_ref[n * blk_rows:(n + 1) * blk_rows, 0:LANES] = kb
            kaug_ref[n * blk_rows:(n + 1) * blk_rows, LANES:] = jnp.where(lane == n, 1.0, 0.0).astype(BF16)
            kmean_ref[n:n + 1, :] = jnp.mean(kb.astype(F32), axis=0, keepdims=True)

    q = q_ref[...]
    lane_q = lax.broadcasted_iota(jnp.int32, q.shape, 1)
    row = lax.broadcasted_iota(jnp.int32, (blk_rows, blk_rows), 0)
    col = lax.broadcasted_iota(jnp.int32, (blk_rows, blk_rows), 1)
    diag = pl.multiple_of(qi * blk_rows, blk_rows)
    kd = kaug_ref[pl.ds(diag, blk_rows), :]
    vd = v_ref[pl.ds(diag, blk_rows), :]
    blk = lax.broadcasted_iota(jnp.int32, (16, blk_rows), 0)
    for h in range(2):
        qh = jnp.where(lane_q // HEAD_DIM == h, q, jnp.zeros_like(q))
        gate_t = _dot_nt(kmean_ref[...], qh.astype(F32), precision=lax.Precision.HIGHEST)[0:16]
        rank = _block_rank(gate_t, blk, qi, n_blocks, 0)
        keep = ((blk < qi) & (rank < MOBA_TOPK)) | (blk == qi)
        pen_t = jnp.where(keep, 0.0, MASKED)
        pen_t = jnp.concatenate([pen_t, jnp.zeros((LANES - 16, blk_rows), F32)], axis=0)
        qaug = jnp.concatenate([qh, pen_t.T.astype(BF16)], axis=1)
        qaug_ref[h] = qaug
        s = _dot_nt(qaug, kd)
        s = jnp.where(col <= row, s, MASKED)
        m = jnp.max(s, axis=1, keepdims=True)
        p = jnp.exp(s - m)
        m_ref[h] = m
        l_ref[h] = jnp.sum(p, axis=1, keepdims=True)
        acc_ref[h] = _dot(p.astype(BF16), vd)

    def body(j, carry):
        off = pl.multiple_of(j * blk_rows, blk_rows)
        kj = kaug_ref[pl.ds(off, blk_rows), :]
        vj = v_ref[pl.ds(off, blk_rows), :]
        for h in range(2):
            s = _dot_nt(qaug_ref[h], kj)
            m_prev = m_ref[h]
            m_new = jnp.maximum(m_prev, jnp.max(s, axis=1, keepdims=True))
            alpha = jnp.exp(m_prev - m_new)
            p = jnp.exp(s - m_new)
            l_ref[h] = alpha * l_ref[h] + jnp.sum(p, axis=1, keepdims=True)
            acc_ref[h] = alpha * acc_ref[h] + _dot(p.astype(BF16), vj)
            m_ref[h] = m_new
        return carry

    lax.fori_loop(0, qi, body, 0)
    out = jnp.where(lane_q < HEAD_DIM, acc_ref[0] / l_ref[0], acc_ref[1] / l_ref[1])
    o_ref[...] = out.astype(o_ref.dtype)


def _moba_prompt_call(q, kb, vb, *, seq):
    n = q.shape[0]
    nb = n // seq
    n_blocks = seq // MOBA_BLOCK
    assert n_blocks <= 16
    pairs = MOBA_WIDTH // LANES
    qspec = pl.BlockSpec((MOBA_BLOCK, LANES), lambda b, hp, i: (b * n_blocks + i, hp))
    kvspec = pl.BlockSpec((seq, LANES), lambda b, hp, i: (b, hp))
    return pl.pallas_call(
        functools.partial(_moba_prompt_kernel, n_blocks=n_blocks),
        grid=(nb, pairs, n_blocks), in_specs=[qspec, kvspec, kvspec], out_specs=qspec,
        out_shape=jax.ShapeDtypeStruct((n, MOBA_WIDTH), BF16),
        scratch_shapes=[pltpu.VMEM((seq, 2 * LANES), BF16), pltpu.VMEM((LANES, LANES), F32),
                        pltpu.VMEM((2, MOBA_BLOCK, 2 * LANES), BF16),
                        pltpu.VMEM((2, MOBA_BLOCK, 1), F32), pltpu.VMEM((2, MOBA_BLOCK, 1), F32),
                        pltpu.VMEM((2, MOBA_BLOCK, LANES), F32)],
        compiler_params=_params(("arbitrary", "arbitrary", "arbitrary")), name="moba_prompt",
    )(q, kb, vb)


def _moba_decode_kernel(pt_ref, q_ref, kn_ref, vn_ref, *rest, pps, n_steps, dec_seq):
    del pt_ref
    k_pages = rest[:pps]
    v_pages = rest[pps:2 * pps]
    o_ref = rest[2 * pps]
    qbd_ref, s_ref, p_ref, sown_ref, pown_ref, gate_ref, l_ref, r_ref = rest[2 * pps + 1:]
    ph = pl.program_id(1)
    st = pl.program_id(2)
    rows = MOBA_HEADS * dec_seq
    n_pages = pps * n_steps
    n_blocks = n_pages // 2
    lane = lax.broadcasted_iota(jnp.int32, (rows, LANES), 1)
    rowi = lax.broadcasted_iota(jnp.int32, (rows, LANES), 0)

    @pl.when((ph == 0) & (st == 0))
    def _():
        qt = jnp.concatenate([q_ref[...]] * MOBA_HEADS, axis=0)
        r2 = lax.broadcasted_iota(jnp.int32, qt.shape, 0)
        c2 = lax.broadcasted_iota(jnp.int32, qt.shape, 1)
        qbd = jnp.where(r2 // dec_seq == c2 // HEAD_DIM, qt, 0.0).astype(BF16)
        qbd_ref[...] = qbd
        gate_ref[...] = jnp.zeros(gate_ref.shape, F32)
        kn = jnp.concatenate([kn_ref[...], jnp.zeros((PAGE_SIZE - dec_seq, MOBA_WIDTH), F32)], axis=0)
        sown_ref[...] = _dot_nt(qbd, kn.astype(BF16))

    @pl.when(ph == 0)
    def _():
        qbd = qbd_ref[...]
        for i in range(0, pps, 2):
            s0 = _dot(qbd, k_pages[i][...].astype(BF16))
            s1 = _dot(qbd, k_pages[i + 1][...].astype(BF16))
            pg = st * pps + i
            s_ref[pg] = s0
            s_ref[pg + 1] = s1
            mean = jnp.sum(s0 + s1, axis=1, keepdims=True) * (1.0 / MOBA_BLOCK)
            gate_ref[...] = jnp.where(lane == pg // 2, mean, gate_ref[...])

    @pl.when((ph == 0) & (st == n_steps - 1))
    def _():
        g = gate_ref[...]
        rank = _block_rank(g, lane, n_blocks, n_blocks, 1)
        keep = (lane < n_blocks) & (rank < MOBA_TOPK)
        own_ok = (lane <= rowi % dec_seq) & (lane < dec_seq)
        s_own = jnp.where(own_ok, sown_ref[...], MASKED)
        m_run = s_own
        for n in range(n_blocks):
            kn = keep[:, n:n + 1]
            for pg in (2 * n, 2 * n + 1):
                m_run = jnp.maximum(m_run, jnp.where(kn, s_ref[pg], MASKED))
        m = jnp.max(m_run, axis=1, keepdims=True)
        p_own = jnp.exp(s_own - m)
        pown_ref[...] = p_own.astype(BF16)
        l_run = p_own
        for n in range(n_blocks):
            kn = keep[:, n:n + 1]
            for pg in (2 * n, 2 * n + 1):
                p = jnp.exp(jnp.where(kn, s_ref[pg], MASKED) - m)
                p_ref[pg] = p.astype(BF16)
                l_run = l_run + p
        l_ref[...] = jnp.sum(l_run, axis=1, keepdims=True)

    @pl.when((ph == 1) & (st == 0))
    def _():
        vn = jnp.concatenate([vn_ref[...], jnp.zeros((PAGE_SIZE - dec_seq, MOBA_WIDTH), F32)], axis=0)
        r_ref[...] = _dot(pown_ref[...], vn.astype(BF16))

    @pl.when(ph == 1)
    def _():
        acc = r_ref[...]
        for i in range(pps):
            acc = acc + _dot_nt(p_ref[st * pps + i], v_pages[i][...].astype(BF16))
        r_ref[...] = acc

    @pl.when((ph == 1) & (st == n_steps - 1))
    def _():
        r = r_ref[...] / l_ref[...]
        c2 = lax.broadcasted_iota(jnp.int32, (dec_seq, MOBA_WIDTH), 1)
        out = jnp.zeros((dec_seq, MOBA_WIDTH), F32)
        for h in range(MOBA_HEADS):
            out = jnp.where(c2 // HEAD_DIM == h, r[h * dec_seq:(h + 1) * dec_seq, :], out)
        o_ref[...] = out


def _moba_decode_call(q, k_new, v_new, cache_k, cache_v, page_table, *, page_base, dec_seq, pps):
    n = q.shape[0]
    nb = n // dec_seq
    n_pages = page_table.shape[1]
    assert n_pages % pps == 0 and pps % 2 == 0 and dec_seq % 8 == 0 and dec_seq <= PAGE_SIZE
    n_steps = n_pages // pps
    n_blocks = n_pages // 2
    assert n_blocks <= LANES
    rows = MOBA_HEADS * dec_seq
    base = page_base

    def k_map(i):
        def f(b, ph, st, pt):
            step = jnp.where(ph == 0, st, n_steps - 1)
            return (base + pt[b, step * pps + i], 0, 0)
        return f

    def v_map(i):
        def f(b, ph, st, pt):
            step = jnp.where(ph == 0, 0, st)
            return (base + pt[b, step * pps + i], 0, 0)
        return f

    tok = pl.BlockSpec((dec_seq, MOBA_WIDTH), lambda b, ph, st, pt: (b, 0))
    page = (None, MOBA_WIDTH, PAGE_SIZE)
    in_specs = ([tok, tok, tok] + [pl.BlockSpec(page, k_map(i)) for i in range(pps)]
                + [pl.BlockSpec(page, v_map(i)) for i in range(pps)])
    grid_spec = pltpu.PrefetchScalarGridSpec(
        num_scalar_prefetch=1, grid=(nb, 2, n_steps), in_specs=in_specs, out_specs=tok,
        scratch_shapes=[pltpu.VMEM((rows, MOBA_WIDTH), BF16),
                        pltpu.VMEM((n_pages, rows, LANES), F32),
                        pltpu.VMEM((n_pages, rows, LANES), BF16),
                        pltpu.VMEM((rows, LANES), F32), pltpu.VMEM((rows, LANES), BF16),
                        pltpu.VMEM((rows, LANES), F32), pltpu.VMEM((rows, 1), F32),
                        pltpu.VMEM((rows, MOBA_WIDTH), F32)])
    return pl.pallas_call(
        functools.partial(_moba_decode_kernel, pps=pps, n_steps=n_steps, dec_seq=dec_seq),
        grid_spec=grid_spec, out_shape=jax.ShapeDtypeStruct((n, MOBA_WIDTH), F32),
        compiler_params=_params(("arbitrary", "arbitrary", "arbitrary")), name="moba_decode",
    )(page_table, q, k_new, v_new, *([cache_k] * pps), *([cache_v] * pps))


def _proj1_kernel(yab_ref, ym_ref, x_ref, wo_ref, g_ref, b_ref, wq_ref, x1_ref, q2_ref, *, alpha, q_scale):
    half = yab_ref.shape[1]
    mix = _dot(yab_ref[...].astype(BF16), wo_ref[0:half, :]) + _dot(ym_ref[...].astype(BF16), wo_ref[half:, :])
    x1 = _layer_norm(alpha * x_ref[...] + mix, g_ref[...], b_ref[...])
    x1_ref[...] = x1
    q2_ref[...] = (_dot(x1.astype(BF16), wq_ref[...]) * q_scale).astype(q2_ref.dtype)


def _proj1_call(yab, ym, x, w_out, g, b, w_q, *, tm, alpha, q_scale, q_dtype):
    n, d = x.shape
    row = lambda i: (i, 0)
    return pl.pallas_call(
        functools.partial(_proj1_kernel, alpha=alpha, q_scale=q_scale),
        grid=(n // tm,),
        in_specs=[pl.BlockSpec((tm, yab.shape[1]), row), pl.BlockSpec((tm, ym.shape[1]), row),
                  pl.BlockSpec((tm, d), row), _resident(w_out.shape), _resident(g.shape), _resident(b.shape),
                  _resident(w_q.shape)],
        out_specs=(pl.BlockSpec((tm, d), row), pl.BlockSpec((tm, d), row)),
        out_shape=(jax.ShapeDtypeStruct((n, d), F32), jax.ShapeDtypeStruct((n, d), q_dtype)),
        compiler_params=_params(("arbitrary",)), name="proj1",
    )(yab, ym, x, w_out, g, b, w_q)


def _xattn_kernel(q_ref, mk_ref, mv_ref, o_ref):
    dh = q_ref.shape[1] // X_HEADS
    for h in range(X_HEADS):
        sl = slice(h * dh, (h + 1) * dh)
        s = _dot_nt(q_ref[:, sl].astype(BF16), mk_ref[:, sl].astype(BF16))
        m = jnp.max(s, axis=1, keepdims=True)
        p = jnp.exp(s - m)
        l = jnp.sum(p, axis=1, keepdims=True)
        o_ref[:, sl] = (_dot(p.astype(BF16), mv_ref[:, sl].astype(BF16)) / l).astype(o_ref.dtype)


def _xattn_call(q2, mem_k, mem_v, k_index, v_index, *, rows_per_batch, tm):
    n, d = q2.shape
    nb = n // rows_per_batch
    n_mem = mem_k.shape[1] // nb
    tiles = rows_per_batch // tm
    qspec = pl.BlockSpec((tm, d), lambda b, t: (b * tiles + t, 0))
    return pl.pallas_call(
        _xattn_kernel, grid=(nb, tiles),
        in_specs=[qspec, pl.BlockSpec((None, n_mem, d), lambda b, t: (k_index, b, 0)),
                  pl.BlockSpec((None, n_mem, d), lambda b, t: (v_index, b, 0))],
        out_specs=qspec, out_shape=jax.ShapeDtypeStruct((n, d), q2.dtype),
        compiler_params=_params(("arbitrary", "arbitrary")), name="xattn",
    )(q2, mem_k, mem_v)


def _proj2_ffn_kernel(o_ref, x1_ref, wo_ref, g2_ref, b2_ref, wg_ref, wu_ref, wd_ref, g3_ref, b3_ref, out_ref,
                      *, alpha, ff_chunk):
    x2 = _layer_norm(alpha * x1_ref[...] + _dot(o_ref[...].astype(BF16), wo_ref[...]), g2_ref[...], b2_ref[...])
    xb = x2.astype(BF16)
    d_ff = wg_ref.shape[1]
    y = jnp.zeros(x2.shape, F32)
    for c in range(d_ff // ff_chunk):
        sl = slice(c * ff_chunk, (c + 1) * ff_chunk)
        gate = _dot(xb, wg_ref[:, sl])
        up = _dot(xb, wu_ref[:, sl])
        hid = gate * (1.0 / (1.0 + jnp.exp(-gate))) * up
        y = y + _dot(hid.astype(BF16), wd_ref[sl, :])
    out_ref[...] = _layer_norm(alpha * x2 + y, g3_ref[...], b3_ref[...])


def _proj2_ffn_call(o, x1, w_o, g2, b2, w_g, w_u, w_d, g3, b3, *, tm, alpha):
    n, d = x1.shape
    row = lambda i: (i, 0)
    consts = (w_o, g2, b2, w_g, w_u, w_d, g3, b3)
    return pl.pallas_call(
        functools.partial(_proj2_ffn_kernel, alpha=alpha, ff_chunk=256),
        grid=(n // tm,),
        in_specs=[pl.BlockSpec((tm, d), row), pl.BlockSpec((tm, d), row)] + [_resident(c.shape) for c in consts],
        out_specs=pl.BlockSpec((tm, d), row), out_shape=jax.ShapeDtypeStruct((n, d), F32),
        compiler_params=_params(("arbitrary",)), name="proj2_ffn",
    )(o, x1, *consts)


def _memkv_kernel(x_ref, w_ref, o_ref):
    o_ref[...] = _dot(x_ref[...].astype(BF16), w_ref[...])


def _memkv_call(mem, w_stack, *, tm):
    n, d = mem.shape
    g = w_stack.shape[0]
    return pl.pallas_call(
        _memkv_kernel, grid=(g, n // tm),
        in_specs=[pl.BlockSpec((tm, d), lambda j, i: (i, 0)), pl.BlockSpec((None, d, d), lambda j, i: (j, 0, 0))],
        out_specs=pl.BlockSpec((None, tm, d), lambda j, i: (j, i, 0)),
        out_shape=jax.ShapeDtypeStruct((g, n, d), F32),
        compiler_params=_params(("arbitrary", "arbitrary")), name="memkv",
    )(mem, w_stack)


def _block_diag(blocks):
    g, c, _ = blocks.shape
    eye = jnp.eye(g, dtype=blocks.dtype)
    return (eye[:, None, :, None] * blocks[:, :, None, :]).reshape(g * c, g * c)


def kernel(x_prompt, x_sample, mem_prompt, state_pool, cache_k, cache_v, cache_mem_k, cache_mem_v, page_table,
           w_in, pool_w, pool_scale, sg_norm_g, sg_norm_b, sg_w, sg_b, w_out, ln1_g, ln1_b, xq_w, xk_w, xv_w,
           xo_w, ln2_g, ln2_b, ffn_gate, ffn_up, ffn_down, ln3_g, ln3_b):
    n_layers = w_in.shape[0]
    bp, seq, d = x_prompt.shape
    bs, dec_seq, _ = x_sample.shape
    n_mem = mem_prompt.shape[1]
    n_phys = cache_k.shape[1]
    past_len = page_table.shape[1] * PAGE_SIZE
    assert past_len % MOBA_BLOCK == 0 and dec_seq <= MOBA_BLOCK and seq % ROW_TILE == 0
    alpha = (2 * n_layers) ** 0.25
    q_scale = (d // X_HEADS) ** -0.5

    xp = x_prompt.reshape(bp * seq, d)
    xs = x_sample.reshape(bs * dec_seq, d)
    paged = lambda c: jnp.transpose(c, (0, 1, 3, 4, 2)).reshape(n_layers * n_phys, MOBA_WIDTH, PAGE_SIZE)
    cache_k2, cache_v2 = paged(cache_k), paged(cache_v)
    cmk = cache_mem_k.reshape(n_layers, bs * n_mem, d)
    cmv = cache_mem_v.reshape(n_layers, bs * n_mem, d)
    hist16 = jnp.pad(state_pool, ((0, 0), (0, 0), (1, 0), (0, 0)))

    vec = lambda p: p.reshape(n_layers, 1, -1)
    w_in_b, w_out_b, xq_b, xo_b = (w.astype(BF16) for w in (w_in, w_out, xq_w, xo_w))
    wg_b, wu_b, wd_b = (w.astype(BF16) for w in (ffn_gate, ffn_up, ffn_down))
    pw_bd = jnp.stack([_block_diag(pool_w[l]) for l in range(n_layers)]).astype(BF16)
    causal = jnp.tril(jnp.ones((SG_CHUNK, SG_CHUNK), bool))
    sgw_p = jnp.where(causal, sg_w, 0.0).astype(BF16)
    sgb_p = jnp.repeat(jnp.swapaxes(sg_b, 1, 2), HEAD_DIM, axis=2)
    eye_s = jnp.eye(bs, dtype=F32)
    w_dec = jnp.where(causal[:dec_seq, :dec_seq], sg_w[:, :, :dec_seq, :dec_seq], 0.0)
    sgw_s = (eye_s[None, None, :, None, :, None] * w_dec[:, :, None, :, None, :]).reshape(
        n_layers, SG_HEADS, bs * dec_seq, bs * dec_seq).astype(BF16)
    sgb_s = jnp.tile(sgb_p[:, :dec_seq, :], (1, bs, 1))
    kv_stack = jnp.concatenate([xk_w, xv_w], axis=0).astype(BF16)

    memkv = _memkv_call(mem_prompt.reshape(bp * n_mem, d), kv_stack, tm=ROW_TILE)

    pool_p, pool_s, kp_l, vp_l, ks_l, vs_l, sgv_l = [], [], [], [], [], [], []
    for l in range(n_layers):
        ln = lambda p: vec(p)[l]
        yab, q, k, v, kb, vb, hist = _in_prompt_call(
            xp, w_in_b[l], pw_bd[l], ln(pool_scale), ln(sg_norm_g), ln(sg_norm_b), sgw_p[l], sgb_p[l],
            seq=seq, tm=ROW_TILE)
        ym = _moba_prompt_call(q, kb, vb, seq=seq)
        x1, q2 = _proj1_call(yab, ym, xp, w_out_b[l], ln(ln1_g), ln(ln1_b), xq_b[l], tm=ROW_TILE, alpha=alpha,
                             q_scale=q_scale, q_dtype=BF16)
        o = _xattn_call(q2, memkv, memkv, l, n_layers + l, rows_per_batch=seq, tm=ROW_TILE)
        xp = _proj2_ffn_call(o, x1, xo_b[l], ln(ln2_g), ln(ln2_b), wg_b[l], wu_b[l], wd_b[l], ln(ln3_g),
                             ln(ln3_b), tm=ROW_TILE, alpha=alpha)
        pool_p.append(hist[:, 1:, :]); kp_l.append(k); vp_l.append(v)
        yab, q, k, v, hist, sgv = _in_sample_call(
            xs, hist16[l], w_in_b[l], pw_bd[l], ln(pool_scale), ln(sg_norm_g), ln(sg_norm_b), sgw_s[l], sgb_s[l],
            dec_seq=dec_seq, pos0=past_len)
        ym = _moba_decode_call(q, k, v, cache_k2, cache_v2, page_table, page_base=l * n_phys, dec_seq=dec_seq, pps=8)
        x1, q2 = _proj1_call(yab, ym, xs, w_out_b[l], ln(ln1_g), ln(ln1_b), xq_b[l], tm=bs * dec_seq, alpha=alpha,
                             q_scale=q_scale, q_dtype=F32)
        o = _xattn_call(q2, cmk, cmv, l, l, rows_per_batch=dec_seq, tm=dec_seq)
        xs = _proj2_ffn_call(o, x1, xo_b[l], ln(ln2_g), ln(ln2_b), wg_b[l], wu_b[l], wd_b[l], ln(ln3_g),
                             ln(ln3_b), tm=bs * dec_seq, alpha=alpha)
        pool_s.append(hist[:, 1:, :]); ks_l.append(k); vs_l.append(v); sgv_l.append(sgv)

    heads = lambda a, nb_, s_: a.reshape(nb_, s_, MOBA_HEADS, HEAD_DIM)
    mem_shape = (n_layers, bp, n_mem, X_HEADS, d // X_HEADS)
    return (xp.reshape(bp, seq, d), xs.reshape(bs, dec_seq, d),
            jnp.stack(pool_p), jnp.stack(pool_s),
            jnp.stack([heads(a, bp, seq) for a in kp_l]), jnp.stack([heads(a, bp, seq) for a in vp_l]),
            jnp.stack([heads(a, bs, dec_seq) for a in ks_l]), jnp.stack([heads(a, bs, dec_seq) for a in vs_l]),
            jnp.stack([a.reshape(bs, dec_seq, SG_WIDTH) for a in sgv_l]),
            memkv[:n_layers].reshape(mem_shape), memkv[n_layers:].reshape(mem_shape))
```

```python
import functools

import jax
import jax.numpy as jnp
from jax import lax
from jax.experimental import pallas as pl
from jax.experimental.pallas import tpu as pltpu

F32 = jnp.float32
BF16 = jnp.bfloat16

HEAD_DIM = 64
POOL_WIDTH = 256
POOL_WINDOWS = (2, 4, 8, 16)
POOL_GROUP_DIM = POOL_WIDTH // len(POOL_WINDOWS)
POOL_HIST = max(POOL_WINDOWS) - 1
HIST_ROWS = POOL_HIST + 1
SG_WIDTH = 256
SG_HEADS = SG_WIDTH // HEAD_DIM
SG_CHUNK = 128
MOBA_WIDTH = 512
MOBA_HEADS = MOBA_WIDTH // HEAD_DIM
MOBA_BLOCK = 256
MOBA_TOPK = 3
PAGE_SIZE = 128
X_HEADS = 4
LN_EPS = 1e-5
MASKED = -1e30

LANES = 128
VMEM_LIMIT_BYTES = 56 * 1024 * 1024
ROW_TILE = 512
DECODE_PAGES_PER_STEP = 16


def _dot(a, b):
    return jnp.dot(a, b, preferred_element_type=F32)


def _dot_nt(a, b, precision=None):
    return lax.dot_general(a, b, (((1,), (1,)), ((), ())), preferred_element_type=F32, precision=precision)


def _layer_norm(y, g, b):
    mu = jnp.mean(y, axis=-1, keepdims=True)
    d = y - mu
    var = jnp.mean(d * d, axis=-1, keepdims=True)
    return d * lax.rsqrt(var + LN_EPS) * g + b


def _params(sem):
    return pltpu.CompilerParams(dimension_semantics=sem, vmem_limit_bytes=VMEM_LIMIT_BYTES)


def _resident(shape):
    nd = len(shape)
    return pl.BlockSpec(shape, lambda *_: (0,) * nd, pipeline_mode=pl.Buffered(1))


def _window_sums(ext):
    s2 = ext + pltpu.roll(ext, 1, 0)
    s4 = s2 + pltpu.roll(s2, 2, 0)
    s8 = s4 + pltpu.roll(s4, 4, 0)
    s16 = s8 + pltpu.roll(s8, 8, 0)
    lane = lax.broadcasted_iota(jnp.int32, ext.shape, 1)
    g = POOL_GROUP_DIM
    return jnp.where(lane < g, s2, jnp.where(lane < 2 * g, s4, jnp.where(lane < 3 * g, s8, s16)))


def _pool_out(sums, a, pos, pw_bd, scale):
    lane = lax.broadcasted_iota(jnp.int32, a.shape, 1)
    g = POOL_GROUP_DIM
    win = jnp.where(lane < g, 2, jnp.where(lane < 2 * g, 4, jnp.where(lane < 3 * g, 8, 16)))
    cnt = jnp.minimum(win, pos + 1).astype(F32)
    pooled = sums / cnt - a
    return _dot(pooled.astype(BF16), pw_bd) * scale


def _spatial_gate(zb, n_g, n_b, sgw_ref, bias, chunk):
    z = jax.nn.gelu(zb, approximate=True)
    u = z[:, :SG_WIDTH]
    v = _layer_norm(z[:, SG_WIDTH:], n_g, n_b)
    vb = v.astype(BF16)
    lane = lax.broadcasted_iota(jnp.int32, (chunk, LANES), 1)
    rows = []
    for c in range(zb.shape[0] // chunk):
        vc = vb[c * chunk:(c + 1) * chunk]
        cols = []
        for pair in range(SG_HEADS // 2):
            vp = vc[:, pair * LANES:(pair + 1) * LANES]
            s0 = _dot(sgw_ref[2 * pair], vp)
            s1 = _dot(sgw_ref[2 * pair + 1], vp)
            cols.append(jnp.where(lane < HEAD_DIM, s0, s1))
        rows.append(jnp.concatenate(cols, axis=1) + bias)
    s = rows[0] if len(rows) == 1 else jnp.concatenate(rows, axis=0)
    return u * s, v


def _project_qkv(xb, w_ref, q_ref, k_ref, v_ref, kb_ref, transposed):
    c0 = POOL_WIDTH + 2 * SG_WIDTH
    q = _dot(xb, w_ref[:, c0:c0 + MOBA_WIDTH]) * (HEAD_DIM ** -0.5)
    q_ref[...] = q.astype(q_ref.dtype)
    k = _dot(xb, w_ref[:, c0 + MOBA_WIDTH:c0 + 2 * MOBA_WIDTH])
    v = _dot(xb, w_ref[:, c0 + 2 * MOBA_WIDTH:c0 + 3 * MOBA_WIDTH])
    if transposed:
        k_ref[...] = k.T.reshape(k_ref.shape)
        v_ref[...] = v.T.reshape(v_ref.shape)
        kb_ref[...] = k.astype(BF16)
    else:
        k_ref[...] = k
        v_ref[...] = v


def _in_prompt_kernel(x_ref, w_ref, pw_ref, ps_ref, ng_ref, nb_ref, sgw_ref, sgb_ref,
                      yab_ref, q_ref, kt_ref, vt_ref, kb_ref, hist_ref, ext_ref, *, tiles_per_seq):
    tm = x_ref.shape[0]
    t = pl.program_id(0) % tiles_per_seq

    @pl.when(t == 0)
    def _():
        ext_ref[0:HIST_ROWS, :] = jnp.zeros((HIST_ROWS, POOL_WIDTH), F32)

    xb = x_ref[...].astype(BF16)
    a = _dot(xb, w_ref[:, 0:POOL_WIDTH])
    ext_ref[HIST_ROWS:, :] = a
    sums = _window_sums(ext_ref[...])[HIST_ROWS:]
    pos = t * tm + lax.broadcasted_iota(jnp.int32, a.shape, 0)
    yab_ref[:, 0:POOL_WIDTH] = _pool_out(sums, a, pos, pw_ref[...], ps_ref[...]).astype(yab_ref.dtype)
    tail = ext_ref[tm:tm + HIST_ROWS, :]
    hist_ref[...] = tail
    ext_ref[0:HIST_ROWS, :] = tail

    zb = _dot(xb, w_ref[:, POOL_WIDTH:POOL_WIDTH + 2 * SG_WIDTH])
    y_sg, _ = _spatial_gate(zb, ng_ref[...], nb_ref[...], sgw_ref, sgb_ref[...], SG_CHUNK)
    yab_ref[:, POOL_WIDTH:] = y_sg.astype(yab_ref.dtype)
    _project_qkv(xb, w_ref, q_ref, kt_ref, vt_ref, kb_ref, True)


def _in_prompt_call(x, w_in, pw_bd, pscale, n_g, n_b, sgw, sgb, *, seq, tm):
    n, d = x.shape
    nb = n // seq
    tiles_per_seq = seq // tm
    row = lambda i: (i, 0)
    tok_minor = lambda i: (i // tiles_per_seq, 0, 0, i % tiles_per_seq)
    out_shape = (
        jax.ShapeDtypeStruct((n, POOL_WIDTH + SG_WIDTH), BF16),
        jax.ShapeDtypeStruct((n, MOBA_WIDTH), BF16),
        jax.ShapeDtypeStruct((nb, MOBA_HEADS, HEAD_DIM, seq), F32),
        jax.ShapeDtypeStruct((nb, MOBA_HEADS, HEAD_DIM, seq), F32),
        jax.ShapeDtypeStruct((n, MOBA_WIDTH), BF16),
        jax.ShapeDtypeStruct((nb, HIST_ROWS, POOL_WIDTH), F32),
    )
    out_specs = (
        pl.BlockSpec((tm, POOL_WIDTH + SG_WIDTH), row),
        pl.BlockSpec((tm, MOBA_WIDTH), row),
        pl.BlockSpec((None, MOBA_HEADS, HEAD_DIM, tm), tok_minor),
        pl.BlockSpec((None, MOBA_HEADS, HEAD_DIM, tm), tok_minor),
        pl.BlockSpec((tm, MOBA_WIDTH), row),
        pl.BlockSpec((None, HIST_ROWS, POOL_WIDTH), lambda i: (i // tiles_per_seq, 0, 0)),
    )
    in_specs = [pl.BlockSpec((tm, d), row), _resident(w_in.shape), _resident(pw_bd.shape),
                _resident(pscale.shape), _resident(n_g.shape), _resident(n_b.shape),
                _resident(sgw.shape), _resident(sgb.shape)]
    return pl.pallas_call(
        functools.partial(_in_prompt_kernel, tiles_per_seq=tiles_per_seq),
        grid=(n // tm,), in_specs=in_specs, out_specs=out_specs, out_shape=out_shape,
        scratch_shapes=[pltpu.VMEM((HIST_ROWS + tm, POOL_WIDTH), F32)],
        compiler_params=_params(("arbitrary",)), name="in_prompt",
    )(x, w_in, pw_bd, pscale, n_g, n_b, sgw, sgb)


def _in_sample_kernel(x_ref, hist_in_ref, w_ref, pw_ref, ps_ref, ng_ref, nb_ref, sgw_ref, sgb_ref,
                      yab_ref, q_ref, k_ref, v_ref, hist_ref, sgv_ref, *, dec_seq, pos0):
    n = x_ref.shape[0]
    nb = n // dec_seq
    seg = HIST_ROWS + dec_seq
    xb = x_ref[...].astype(BF16)
    a = _dot(xb, w_ref[:, 0:POOL_WIDTH])
    ext = jnp.concatenate([hist_in_ref[...], a.reshape(nb, dec_seq, POOL_WIDTH)], axis=1)
    sums = _window_sums(ext.reshape(nb * seg, POOL_WIDTH)).reshape(nb, seg, POOL_WIDTH)
    sums = sums[:, HIST_ROWS:, :].reshape(n, POOL_WIDTH)
    pos = pos0 + lax.broadcasted_iota(jnp.int32, a.shape, 0) % dec_seq
    yab_ref[:, 0:POOL_WIDTH] = _pool_out(sums, a, pos, pw_ref[...], ps_ref[...]).astype(yab_ref.dtype)
    hist_ref[...] = ext[:, seg - HIST_ROWS:, :]

    zb = _dot(xb, w_ref[:, POOL_WIDTH:POOL_WIDTH + 2 * SG_WIDTH])
    y_sg, v_n = _spatial_gate(zb, ng_ref[...], nb_ref[...], sgw_ref, sgb_ref[...], n)
    yab_ref[:, POOL_WIDTH:] = y_sg.astype(yab_ref.dtype)
    sgv_ref[...] = v_n
    _project_qkv(xb, w_ref, q_ref, k_ref, v_ref, None, False)


def _in_sample_call(x, hist16, w_in, pw_bd, pscale, n_g, n_b, sgw_bd, sgb_t, *, dec_seq, pos0):
    n, d = x.shape
    nb = n // dec_seq
    out_shape = (
        jax.ShapeDtypeStruct((n, POOL_WIDTH + SG_WIDTH), BF16),
        jax.ShapeDtypeStruct((n, MOBA_WIDTH), F32),
        jax.ShapeDtypeStruct((n, MOBA_WIDTH), F32),
        jax.ShapeDtypeStruct((n, MOBA_WIDTH), F32),
        jax.ShapeDtypeStruct((nb, HIST_ROWS, POOL_WIDTH), F32),
        jax.ShapeDtypeStruct((n, SG_WIDTH), F32),
    )
    args = (x, hist16, w_in, pw_bd, pscale, n_g, n_b, sgw_bd, sgb_t)
    return pl.pallas_call(
        functools.partial(_in_sample_kernel, dec_seq=dec_seq, pos0=pos0),
        grid=(1,), in_specs=[_resident(a.shape) for a in args],
        out_specs=tuple(_resident(s.shape) for s in out_shape), out_shape=out_shape,
        compiler_params=_params(("arbitrary",)), name="in_sample",
    )(*args)


def _block_rank(g, blk, n_valid, n_blocks, axis):
    rank = jnp.zeros(g.shape, jnp.int32)
    for m in range(n_blocks):
        gm = lax.slice_in_dim(g, m, m + 1, axis=axis)
        beats = (gm > g) | ((gm == g) & (m < blk))
        rank = rank + jnp.where(beats & (m < n_valid), 1, 0)
    return rank


GATE_ROWS = 16


def _moba_prompt_kernel(q_ref, k_ref, vt_ref, o_ref, kaug_ref, vaug_ref, kmean_ref, qaug_ref, s_ref, m_ref, acc_ref,
                        *, n_blocks):
    blk_rows = MOBA_BLOCK
    qi = pl.program_id(2)

    @pl.when(qi == 0)
    def _():
        lane = lax.broadcasted_iota(jnp.int32, (blk_rows, LANES), 1)
        sub = lax.broadcasted_iota(jnp.int32, (LANES, blk_rows), 0)
        kmean_ref[...] = jnp.zeros(kmean_ref.shape, F32)
        for n in range(n_blocks):
            kb = k_ref[n * blk_rows:(n + 1) * blk_rows, :]
            kaug_ref[n * blk_rows:(n + 1) * blk_rows, 0:LANES] = kb
            kaug_ref[n * blk_rows:(n + 1) * blk_rows, LANES:] = jnp.where(lane == n, 1.0, 0.0).astype(BF16)
            kmean_ref[n:n + 1, :] = jnp.mean(kb.astype(F32), axis=0, keepdims=True)
            vt = vt_ref[:, :, n * blk_rows:(n + 1) * blk_rows].reshape(LANES, blk_rows)
            vaug_ref[n, 0] = jnp.where(sub < HEAD_DIM, vt, 1.0).astype(BF16)
            vaug_ref[n, 1] = jnp.where(sub >= HEAD_DIM, vt, 1.0).astype(BF16)

    q = q_ref[...]
    lane_q = lax.broadcasted_iota(jnp.int32, q.shape, 1)
    key_i = lax.broadcasted_iota(jnp.int32, (blk_rows, blk_rows), 0)
    qry_i = lax.broadcasted_iota(jnp.int32, (blk_rows, blk_rows), 1)
    diag = pl.multiple_of(qi * blk_rows, blk_rows)
    kd = kaug_ref[pl.ds(diag, blk_rows), 0:LANES]
    blk = lax.broadcasted_iota(jnp.int32, (GATE_ROWS, blk_rows), 0)
    for h in range(2):
        qh = jnp.where(lane_q // HEAD_DIM == h, q, jnp.zeros_like(q))
        gate_t = _dot_nt(kmean_ref[...], qh.astype(F32), precision=lax.Precision.HIGHEST)[0:GATE_ROWS]
        rank = _block_rank(gate_t, blk, qi, n_blocks, 0)
        keep = (blk < qi) & (rank < MOBA_TOPK)
        pen_t = jnp.where(keep, 0.0, MASKED)
        pen_t = jnp.concatenate([pen_t, jnp.zeros((LANES - GATE_ROWS, blk_rows), F32)], axis=0)
        qaug_ref[h] = jnp.concatenate([qh, pen_t.T.astype(BF16)], axis=1)
        st = jnp.where(key_i <= qry_i, _dot_nt(kd, qh), MASKED)
        m = jnp.max(st, axis=0, keepdims=True)
        m_ref[h] = m
        acc_ref[h] = _dot(vaug_ref[qi, h], jnp.exp(st - m).astype(BF16))

    def scores(t, slot):
        off = pl.multiple_of(t * 2 * blk_rows, 2 * blk_rows)
        kj = kaug_ref[pl.ds(off, 2 * blk_rows), :]
        for h in range(2):
            s_ref[slot, h] = _dot_nt(kj, qaug_ref[h])

    def accumulate(t, slot):
        for h in range(2):
            st = s_ref[slot, h]
            m_prev = m_ref[h]
            m_new = jnp.maximum(m_prev, jnp.max(st, axis=0, keepdims=True))
            p = jnp.exp(st - m_new).astype(BF16)
            pv = _dot(vaug_ref[2 * t, h], p[0:blk_rows]) + _dot(vaug_ref[2 * t + 1, h], p[blk_rows:])
            acc_ref[h] = jnp.exp(m_prev - m_new) * acc_ref[h] + pv
            m_ref[h] = m_new

    n_pairs = (qi + 1) // 2
    scores(0, 0)

    def body(t, carry):
        scores(t + 1, (t + 1) % 2)
        accumulate(t, t % 2)
        return carry

    lax.fori_loop(0, n_pairs - 1, body, 0)

    @pl.when(n_pairs > 0)
    def _():
        accumulate(n_pairs - 1, (n_pairs - 1) % 2)

    a0 = acc_ref[0]
    a1 = acc_ref[1]
    out_t = jnp.concatenate([a0[0:HEAD_DIM] / a0[HEAD_DIM:], a1[HEAD_DIM:] / a1[0:HEAD_DIM]], axis=0)
    o_ref[...] = out_t.T.astype(o_ref.dtype)


def _moba_prompt_call(q, kb, vt, *, seq):
    n = q.shape[0]
    nb = n // seq
    n_blocks = seq // MOBA_BLOCK
    assert n_blocks <= GATE_ROWS and n_blocks % 2 == 0
    pairs = MOBA_WIDTH // LANES
    qspec = pl.BlockSpec((MOBA_BLOCK, LANES), lambda b, hp, i: (b * n_blocks + i, hp))
    kspec = pl.BlockSpec((seq, LANES), lambda b, hp, i: (b, hp))
    vspec = pl.BlockSpec((None, 2, HEAD_DIM, seq), lambda b, hp, i: (b, hp, 0, 0))
    return pl.pallas_call(
        functools.partial(_moba_prompt_kernel, n_blocks=n_blocks),
        grid=(nb, pairs, n_blocks), in_specs=[qspec, kspec, vspec], out_specs=qspec,
        out_shape=jax.ShapeDtypeStruct((n, MOBA_WIDTH), BF16),
        scratch_shapes=[pltpu.VMEM((seq, 2 * LANES), BF16),
                        pltpu.VMEM((n_blocks, 2, LANES, MOBA_BLOCK), BF16),
                        pltpu.VMEM((LANES, LANES), F32),
                        pltpu.VMEM((2, MOBA_BLOCK, 2 * LANES), BF16),
                        pltpu.VMEM((2, 2, 2 * MOBA_BLOCK, MOBA_BLOCK), F32),
                        pltpu.VMEM((2, 1, MOBA_BLOCK), F32),
                        pltpu.VMEM((2, LANES, MOBA_BLOCK), F32)],
        compiler_params=_params(("arbitrary", "arbitrary", "arbitrary")), name="moba_prompt",
    )(q, kb, vt)


def _moba_decode_kernel(pt_ref, q_ref, kn_ref, vn_ref, *rest, pps, n_steps, dec_seq):
    del pt_ref
    k_pages = rest[:pps]
    v_pages = rest[pps:2 * pps]
    o_ref = rest[2 * pps]
    qbd_ref, s_ref, p_ref, sown_ref, pown_ref, gate_ref, l_ref, r_ref = rest[2 * pps + 1:]
    ph = pl.program_id(1)
    st = pl.program_id(2)
    rows = MOBA_HEADS * dec_seq
    n_pages = pps * n_steps
    n_blocks = n_pages // 2
    lane = lax.broadcasted_iota(jnp.int32, (rows, LANES), 1)
    rowi = lax.broadcasted_iota(jnp.int32, (rows, LANES), 0)

    @pl.when((ph == 0) & (st == 0))
    def _():
        qt = jnp.concatenate([q_ref[...]] * MOBA_HEADS, axis=0)
        r2 = lax.broadcasted_iota(jnp.int32, qt.shape, 0)
        c2 = lax.broadcasted_iota(jnp.int32, qt.shape, 1)
        qbd = jnp.where(r2 // dec_seq == c2 // HEAD_DIM, qt, 0.0).astype(BF16)
        qbd_ref[...] = qbd
        gate_ref[...] = jnp.zeros(gate_ref.shape, F32)
        kn = jnp.concatenate([kn_ref[...], jnp.zeros((PAGE_SIZE - dec_seq, MOBA_WIDTH), F32)], axis=0)
        sown_ref[...] = _dot_nt(qbd, kn.astype(BF16))

    @pl.when(ph == 0)
    def _():
        qbd = qbd_ref[...]
        for i in range(0, pps, 2):
            s0 = _dot(qbd, k_pages[i][...].astype(BF16))
            s1 = _dot(qbd, k_pages[i + 1][...].astype(BF16))
            pg = st * pps + i
            s_ref[pg] = s0
            s_ref[pg + 1] = s1
            mean = jnp.sum(s0 + s1, axis=1, keepdims=True) * (1.0 / MOBA_BLOCK)
            gate_ref[...] = jnp.where(lane == pg // 2, mean, gate_ref[...])

    @pl.when((ph == 0) & (st == n_steps - 1))
    def _():
        g = gate_ref[...]
        rank = _block_rank(g, lane, n_blocks, n_blocks, 1)
        keep = (lane < n_blocks) & (rank < MOBA_TOPK)
        own_ok = (lane <= rowi % dec_seq) & (lane < dec_seq)
        s_own = jnp.where(own_ok, sown_ref[...], MASKED)
        m_run = s_own
        for n in range(n_blocks):
            kn = keep[:, n:n + 1]
            for pg in (2 * n, 2 * n + 1):
                m_run = jnp.maximum(m_run, jnp.where(kn, s_ref[pg], MASKED))
        m = jnp.max(m_run, axis=1, keepdims=True)
        p_own = jnp.exp(s_own - m)
        pown_ref[...] = p_own.astype(BF16)
        l_run = p_own
        for n in range(n_blocks):
            kn = keep[:, n:n + 1]
            for pg in (2 * n, 2 * n + 1):
                p = jnp.exp(jnp.where(kn, s_ref[pg], MASKED) - m)
                p_ref[pg] = p.astype(BF16)
                l_run = l_run + p
        l_ref[...] = jnp.sum(l_run, axis=1, keepdims=True)

    @pl.when((ph == 1) & (st == 0))
    def _():
        vn = jnp.concatenate([vn_ref[...], jnp.zeros((PAGE_SIZE - dec_seq, MOBA_WIDTH), F32)], axis=0)
        r_ref[...] = _dot(pown_ref[...], vn.astype(BF16))

    @pl.when(ph == 1)
    def _():
        acc = r_ref[...]
        for i in range(pps):
            acc = acc + _dot_nt(p_ref[st * pps + i], v_pages[i][...].astype(BF16))
        r_ref[...] = acc

    @pl.when((ph == 1) & (st == n_steps - 1))
    def _():
        r = r_ref[...] / l_ref[...]
        c2 = lax.broadcasted_iota(jnp.int32, (dec_seq, MOBA_WIDTH), 1)
        out = jnp.zeros((dec_seq, MOBA_WIDTH), F32)
        for h in range(MOBA_HEADS):
            out = jnp.where(c2 // HEAD_DIM == h, r[h * dec_seq:(h + 1) * dec_seq, :], out)
        o_ref[...] = out


def _moba_decode_call(q, k_new, v_new, cache_k, cache_v, page_table, *, page_base, dec_seq, pps):
    n = q.shape[0]
    nb = n // dec_seq
    n_pages = page_table.shape[1]
    assert n_pages % pps == 0 and pps % 2 == 0 and dec_seq % 8 == 0 and dec_seq <= PAGE_SIZE
    n_steps = n_pages // pps
    n_blocks = n_pages // 2
    assert n_blocks <= LANES
    rows = MOBA_HEADS * dec_seq
    base = page_base

    def k_map(i):
        def f(b, ph, st, pt):
            step = jnp.where(ph == 0, st, n_steps - 1)
            return (base + pt[b, step * pps + i], 0, 0)
        return f

    def v_map(i):
        def f(b, ph, st, pt):
            step = jnp.where(ph == 0, 0, st)
            return (base + pt[b, step * pps + i], 0, 0)
        return f

    tok = pl.BlockSpec((dec_seq, MOBA_WIDTH), lambda b, ph, st, pt: (b, 0))
    page = (None, MOBA_WIDTH, PAGE_SIZE)
    in_specs = ([tok, tok, tok] + [pl.BlockSpec(page, k_map(i)) for i in range(pps)]
                + [pl.BlockSpec(page, v_map(i)) for i in range(pps)])
    grid_spec = pltpu.PrefetchScalarGridSpec(
        num_scalar_prefetch=1, grid=(nb, 2, n_steps), in_specs=in_specs, out_specs=tok,
        scratch_shapes=[pltpu.VMEM((rows, MOBA_WIDTH), BF16),
                        pltpu.VMEM((n_pages, rows, LANES), F32),
                        pltpu.VMEM((n_pages, rows, LANES), BF16),
                        pltpu.VMEM((rows, LANES), F32), pltpu.VMEM((rows, LANES), BF16),
                        pltpu.VMEM((rows, LANES), F32), pltpu.VMEM((rows, 1), F32),
                        pltpu.VMEM((rows, MOBA_WIDTH), F32)])
    return pl.pallas_call(
        functools.partial(_moba_decode_kernel, pps=pps, n_steps=n_steps, dec_seq=dec_seq),
        grid_spec=grid_spec, out_shape=jax.ShapeDtypeStruct((n, MOBA_WIDTH), F32),
        compiler_params=_params(("arbitrary", "arbitrary", "arbitrary")), name="moba_decode",
    )(page_table, q, k_new, v_new, *([cache_k] * pps), *([cache_v] * pps))


def _proj1_kernel(yab_ref, ym_ref, x_ref, wo_ref, g_ref, b_ref, wq_ref, x1_ref, q2_ref, *, alpha, q_scale):
    half = yab_ref.shape[1]
    mix = _dot(yab_ref[...].astype(BF16), wo_ref[0:half, :]) + _dot(ym_ref[...].astype(BF16), wo_ref[half:, :])
    x1 = _layer_norm(alpha * x_ref[...] + mix, g_ref[...], b_ref[...])
    x1_ref[...] = x1
    q2_ref[...] = (_dot(x1.astype(BF16), wq_ref[...]) * q_scale).astype(q2_ref.dtype)


def _proj1_call(yab, ym, x, w_out, g, b, w_q, *, tm, alpha, q_scale, q_dtype):
    n, d = x.shape
    row = lambda i: (i, 0)
    return pl.pallas_call(
        functools.partial(_proj1_kernel, alpha=alpha, q_scale=q_scale),
        grid=(n // tm,),
        in_specs=[pl.BlockSpec((tm, yab.shape[1]), row), pl.BlockSpec((tm, ym.shape[1]), row),
                  pl.BlockSpec((tm, d), row), _resident(w_out.shape), _resident(g.shape), _resident(b.shape),
                  _resident(w_q.shape)],
        out_specs=(pl.BlockSpec((tm, d), row), pl.BlockSpec((tm, d), row)),
        out_shape=(jax.ShapeDtypeStruct((n, d), F32), jax.ShapeDtypeStruct((n, d), q_dtype)),
        compiler_params=_params(("arbitrary",)), name="proj1",
    )(yab, ym, x, w_out, g, b, w_q)


def _xattn_kernel(q_ref, mk_ref, mv_ref, o_ref):
    dh = q_ref.shape[1] // X_HEADS
    for h in range(X_HEADS):
        sl = slice(h * dh, (h + 1) * dh)
        s = _dot_nt(q_ref[:, sl].astype(BF16), mk_ref[:, sl].astype(BF16))
        m = jnp.max(s, axis=1, keepdims=True)
        p = jnp.exp(s - m)
        l = jnp.sum(p, axis=1, keepdims=True)
        o_ref[:, sl] = (_dot(p.astype(BF16), mv_ref[:, sl].astype(BF16)) / l).astype(o_ref.dtype)


def _xattn_call(q2, mem_k, mem_v, k_index, v_index, *, rows_per_batch, tm):
    n, d = q2.shape
    nb = n // rows_per_batch
    n_mem = mem_k.shape[1] // nb
    tiles = rows_per_batch // tm
    qspec = pl.BlockSpec((tm, d), lambda b, t: (b * tiles + t, 0))
    return pl.pallas_call(
        _xattn_kernel, grid=(nb, tiles),
        in_specs=[qspec, pl.BlockSpec((None, n_mem, d), lambda b, t: (k_index, b, 0)),
                  pl.BlockSpec((None, n_mem, d), lambda b, t: (v_index, b, 0))],
        out_specs=qspec, out_shape=jax.ShapeDtypeStruct((n, d), q2.dtype),
        compiler_params=_params(("arbitrary", "arbitrary")), name="xattn",
    )(q2, mem_k, mem_v)


def _proj2_ffn_kernel(o_ref, x1_ref, wo_ref, g2_ref, b2_ref, wg_ref, wu_ref, wd_ref, g3_ref, b3_ref, out_ref,
                      *, alpha, ff_chunk):
    x2 = _layer_norm(alpha * x1_ref[...] + _dot(o_ref[...].astype(BF16), wo_ref[...]), g2_ref[...], b2_ref[...])
    xb = x2.astype(BF16)
    d_ff = wg_ref.shape[1]
    y = jnp.zeros(x2.shape, F32)
    for c in range(d_ff // ff_chunk):
        sl = slice(c * ff_chunk, (c + 1) * ff_chunk)
        gate = _dot(xb, wg_ref[:, sl])
        up = _dot(xb, wu_ref[:, sl])
        hid = gate * (1.0 / (1.0 + jnp.exp(-gate))) * up
        y = y + _dot(hid.astype(BF16), wd_ref[sl, :])
    out_ref[...] = _layer_norm(alpha * x2 + y, g3_ref[...], b3_ref[...])


def _proj2_ffn_call(o, x1, w_o, g2, b2, w_g, w_u, w_d, g3, b3, *, tm, alpha):
    n, d = x1.shape
    row = lambda i: (i, 0)
    consts = (w_o, g2, b2, w_g, w_u, w_d, g3, b3)
    return pl.pallas_call(
        functools.partial(_proj2_ffn_kernel, alpha=alpha, ff_chunk=256),
        grid=(n // tm,),
        in_specs=[pl.BlockSpec((tm, d), row), pl.BlockSpec((tm, d), row)] + [_resident(c.shape) for c in consts],
        out_specs=pl.BlockSpec((tm, d), row), out_shape=jax.ShapeDtypeStruct((n, d), F32),
        compiler_params=_params(("arbitrary",)), name="proj2_ffn",
    )(o, x1, *consts)


def _memkv_kernel(x_ref, w_ref, o_ref):
    o_ref[...] = _dot(x_ref[...].astype(BF16), w_ref[...])


def _memkv_call(mem, w_stack, *, tm):
    n, d = mem.shape
    g = w_stack.shape[0]
    return pl.pallas_call(
        _memkv_kernel, grid=(g, n // tm),
        in_specs=[pl.BlockSpec((tm, d), lambda j, i: (i, 0)), pl.BlockSpec((None, d, d), lambda j, i: (j, 0, 0))],
        out_specs=pl.BlockSpec((None, tm, d), lambda j, i: (j, i, 0)),
        out_shape=jax.ShapeDtypeStruct((g, n, d), F32),
        compiler_params=_params(("arbitrary", "arbitrary")), name="memkv",
    )(mem, w_stack)


def _block_diag(blocks):
    g, c, _ = blocks.shape
    eye = jnp.eye(g, dtype=blocks.dtype)
    return (eye[:, None, :, None] * blocks[:, :, None, :]).reshape(g * c, g * c)


def kernel(x_prompt, x_sample, mem_prompt, state_pool, cache_k, cache_v, cache_mem_k, cache_mem_v, page_table,
           w_in, pool_w, pool_scale, sg_norm_g, sg_norm_b, sg_w, sg_b, w_out, ln1_g, ln1_b, xq_w, xk_w, xv_w,
           xo_w, ln2_g, ln2_b, ffn_gate, ffn_up, ffn_down, ln3_g, ln3_b):
    n_layers = w_in.shape[0]
    bp, seq, d = x_prompt.shape
    bs, dec_seq, _ = x_sample.shape
    n_mem = mem_prompt.shape[1]
    n_phys = cache_k.shape[1]
    past_len = page_table.shape[1] * PAGE_SIZE
    assert past_len % MOBA_BLOCK == 0 and dec_seq <= MOBA_BLOCK and seq % ROW_TILE == 0
    alpha = (2 * n_layers) ** 0.25
    q_scale = (d // X_HEADS) ** -0.5

    xp = x_prompt.reshape(bp * seq, d)
    xs = x_sample.reshape(bs * dec_seq, d)
    paged = lambda c: jnp.transpose(c, (0, 1, 3, 4, 2)).reshape(n_layers * n_phys, MOBA_WIDTH, PAGE_SIZE)
    cache_k2, cache_v2 = paged(cache_k), paged(cache_v)
    cmk = cache_mem_k.reshape(n_layers, bs * n_mem, d)
    cmv = cache_mem_v.reshape(n_layers, bs * n_mem, d)
    hist16 = jnp.pad(state_pool, ((0, 0), (0, 0), (1, 0), (0, 0)))

    vec = lambda p: p.reshape(n_layers, 1, -1)
    w_in_b, w_out_b, xq_b, xo_b = (w.astype(BF16) for w in (w_in, w_out, xq_w, xo_w))
    wg_b, wu_b, wd_b = (w.astype(BF16) for w in (ffn_gate, ffn_up, ffn_down))
    pw_bd = jnp.stack([_block_diag(pool_w[l]) for l in range(n_layers)]).astype(BF16)
    causal = jnp.tril(jnp.ones((SG_CHUNK, SG_CHUNK), bool))
    sgw_p = jnp.where(causal, sg_w, 0.0).astype(BF16)
    sgb_p = jnp.repeat(jnp.swapaxes(sg_b, 1, 2), HEAD_DIM, axis=2)
    eye_s = jnp.eye(bs, dtype=F32)
    w_dec = jnp.where(causal[:dec_seq, :dec_seq], sg_w[:, :, :dec_seq, :dec_seq], 0.0)
    sgw_s = (eye_s[None, None, :, None, :, None] * w_dec[:, :, None, :, None, :]).reshape(
        n_layers, SG_HEADS, bs * dec_seq, bs * dec_seq).astype(BF16)
    sgb_s = jnp.tile(sgb_p[:, :dec_seq, :], (1, bs, 1))
    kv_stack = jnp.concatenate([xk_w, xv_w], axis=0).astype(BF16)

    memkv = _memkv_call(mem_prompt.reshape(bp * n_mem, d), kv_stack, tm=ROW_TILE)

    pool_p, pool_s, kp_l, vp_l, ks_l, vs_l, sgv_l = [], [], [], [], [], [], []
    for l in range(n_layers):
        ln = lambda p: vec(p)[l]
        yab, q, kt, vt, kb, hist = _in_prompt_call(
            xp, w_in_b[l], pw_bd[l], ln(pool_scale), ln(sg_norm_g), ln(sg_norm_b), sgw_p[l], sgb_p[l],
            seq=seq, tm=ROW_TILE)
        ym = _moba_prompt_call(q, kb, vt, seq=seq)
        x1, q2 = _proj1_call(yab, ym, xp, w_out_b[l], ln(ln1_g), ln(ln1_b), xq_b[l], tm=ROW_TILE, alpha=alpha,
                             q_scale=q_scale, q_dtype=BF16)
        o = _xattn_call(q2, memkv, memkv, l, n_layers + l, rows_per_batch=seq, tm=ROW_TILE)
        xp = _proj2_ffn_call(o, x1, xo_b[l], ln(ln2_g), ln(ln2_b), wg_b[l], wu_b[l], wd_b[l], ln(ln3_g),
                             ln(ln3_b), tm=ROW_TILE, alpha=alpha)
        pool_p.append(hist[:, 1:, :]); kp_l.append(kt); vp_l.append(vt)
        yab, q, k, v, hist, sgv = _in_sample_call(
            xs, hist16[l], w_in_b[l], pw_bd[l], ln(pool_scale), ln(sg_norm_g), ln(sg_norm_b), sgw_s[l], sgb_s[l],
            dec_seq=dec_seq, pos0=past_len)
        ym = _moba_decode_call(q, k, v, cache_k2, cache_v2, page_table, page_base=l * n_phys, dec_seq=dec_seq,
                               pps=DECODE_PAGES_PER_STEP)
        x1, q2 = _proj1_call(yab, ym, xs, w_out_b[l], ln(ln1_g), ln(ln1_b), xq_b[l], tm=bs * dec_seq, alpha=alpha,
                             q_scale=q_scale, q_dtype=F32)
        o = _xattn_call(q2, cmk, cmv, l, l, rows_per_batch=dec_seq, tm=dec_seq)
        xs = _proj2_ffn_call(o, x1, xo_b[l], ln(ln2_g), ln(ln2_b), wg_b[l], wu_b[l], wd_b[l], ln(ln3_g),
                             ln(ln3_b), tm=bs * dec_seq, alpha=alpha)
        pool_s.append(hist[:, 1:, :]); ks_l.append(k); vs_l.append(v); sgv_l.append(sgv)

    heads = lambda a, nb_, s_: a.reshape(nb_, s_, MOBA_HEADS, HEAD_DIM)
    tok_major = lambda a: jnp.transpose(a, (0, 1, 4, 2, 3))
    mem_shape = (n_layers, bp, n_mem, X_HEADS, d // X_HEADS)
    return (xp.reshape(bp, seq, d), xs.reshape(bs, dec_seq, d),
            jnp.stack(pool_p), jnp.stack(pool_s),
            tok_major(jnp.stack(kp_l)), tok_major(jnp.stack(vp_l)),
            jnp.stack([heads(a, bs, dec_seq) for a in ks_l]), jnp.stack([heads(a, bs, dec_seq) for a in vs_l]),
            jnp.stack([a.reshape(bs, dec_seq, SG_WIDTH) for a in sgv_l]),
            memkv[:n_layers].reshape(mem_shape), memkv[n_layers:].reshape(mem_shape))
```

```python
import functools

import jax
import jax.numpy as jnp
from jax import lax
from jax.experimental import pallas as pl
from jax.experimental.pallas import tpu as pltpu

F32 = jnp.float32
BF16 = jnp.bfloat16

HEAD_DIM = 64
POOL_WIDTH = 256
POOL_WINDOWS = (2, 4, 8, 16)
POOL_GROUP_DIM = POOL_WIDTH // len(POOL_WINDOWS)
POOL_HIST = max(POOL_WINDOWS) - 1
HIST_ROWS = POOL_HIST + 1
SG_WIDTH = 256
SG_HEADS = SG_WIDTH // HEAD_DIM
SG_CHUNK = 128
MOBA_WIDTH = 512
MOBA_HEADS = MOBA_WIDTH // HEAD_DIM
MOBA_BLOCK = 256
MOBA_TOPK = 3
PAGE_SIZE = 128
X_HEADS = 4
LN_EPS = 1e-5
MASKED = -1e30

LANES = 128
VMEM_LIMIT_BYTES = 56 * 1024 * 1024
ROW_TILE = 512
DECODE_PAGES_PER_STEP = 16
MOBA_PAIRS_PER_STEP = 2


def _dot(a, b):
    return jnp.dot(a, b, preferred_element_type=F32)


def _dot_nt(a, b, precision=None):
    return lax.dot_general(a, b, (((1,), (1,)), ((), ())), preferred_element_type=F32, precision=precision)


def _layer_norm(y, g, b):
    mu = jnp.mean(y, axis=-1, keepdims=True)
    d = y - mu
    var = jnp.mean(d * d, axis=-1, keepdims=True)
    return d * lax.rsqrt(var + LN_EPS) * g + b


def _params(sem):
    return pltpu.CompilerParams(dimension_semantics=sem, vmem_limit_bytes=VMEM_LIMIT_BYTES)


def _resident(shape):
    nd = len(shape)
    return pl.BlockSpec(shape, lambda *_: (0,) * nd, pipeline_mode=pl.Buffered(1))


def _window_sums(ext):
    s2 = ext + pltpu.roll(ext, 1, 0)
    s4 = s2 + pltpu.roll(s2, 2, 0)
    s8 = s4 + pltpu.roll(s4, 4, 0)
    s16 = s8 + pltpu.roll(s8, 8, 0)
    lane = lax.broadcasted_iota(jnp.int32, ext.shape, 1)
    g = POOL_GROUP_DIM
    return jnp.where(lane < g, s2, jnp.where(lane < 2 * g, s4, jnp.where(lane < 3 * g, s8, s16)))


def _pool_out(sums, a, pos, pw_bd, scale):
    lane = lax.broadcasted_iota(jnp.int32, a.shape, 1)
    g = POOL_GROUP_DIM
    win = jnp.where(lane < g, 2, jnp.where(lane < 2 * g, 4, jnp.where(lane < 3 * g, 8, 16)))
    cnt = jnp.minimum(win, pos + 1).astype(F32)
    pooled = sums / cnt - a
    return _dot(pooled.astype(BF16), pw_bd) * scale


def _spatial_gate(zb, n_g, n_b, sgw_ref, bias, chunk):
    z = jax.nn.gelu(zb, approximate=True)
    u = z[:, :SG_WIDTH]
    v = _layer_norm(z[:, SG_WIDTH:], n_g, n_b)
    vb = v.astype(BF16)
    lane = lax.broadcasted_iota(jnp.int32, (chunk, LANES), 1)
    rows = []
    for c in range(zb.shape[0] // chunk):
        vc = vb[c * chunk:(c + 1) * chunk]
        cols = []
        for pair in range(SG_HEADS // 2):
            vp = vc[:, pair * LANES:(pair + 1) * LANES]
            s0 = _dot(sgw_ref[2 * pair], vp)
            s1 = _dot(sgw_ref[2 * pair + 1], vp)
            cols.append(jnp.where(lane < HEAD_DIM, s0, s1))
        rows.append(jnp.concatenate(cols, axis=1) + bias)
    s = rows[0] if len(rows) == 1 else jnp.concatenate(rows, axis=0)
    return u * s, v


def _project_qkv(xb, w_ref, q_ref, k_ref, v_ref, kb_ref, transposed):
    c0 = POOL_WIDTH + 2 * SG_WIDTH
    q = _dot(xb, w_ref[:, c0:c0 + MOBA_WIDTH]) * (HEAD_DIM ** -0.5)
    q_ref[...] = q.astype(q_ref.dtype)
    k = _dot(xb, w_ref[:, c0 + MOBA_WIDTH:c0 + 2 * MOBA_WIDTH])
    v = _dot(xb, w_ref[:, c0 + 2 * MOBA_WIDTH:c0 + 3 * MOBA_WIDTH])
    if transposed:
        k_ref[...] = k.T.reshape(k_ref.shape)
        v_ref[...] = v.T.reshape(v_ref.shape)
        kb_ref[...] = k.astype(BF16)
    else:
        k_ref[...] = k
        v_ref[...] = v


def _in_prompt_kernel(x_ref, w_ref, pw_ref, ps_ref, ng_ref, nb_ref, sgw_ref, sgb_ref,
                      yab_ref, q_ref, kt_ref, vt_ref, kb_ref, hist_ref, ext_ref, *, tiles_per_seq):
    tm = x_ref.shape[0]
    t = pl.program_id(0) % tiles_per_seq

    @pl.when(t == 0)
    def _():
        ext_ref[0:HIST_ROWS, :] = jnp.zeros((HIST_ROWS, POOL_WIDTH), F32)

    xb = x_ref[...].astype(BF16)
    a = _dot(xb, w_ref[:, 0:POOL_WIDTH])
    ext_ref[HIST_ROWS:, :] = a
    sums = _window_sums(ext_ref[...])[HIST_ROWS:]
    pos = t * tm + lax.broadcasted_iota(jnp.int32, a.shape, 0)
    yab_ref[:, 0:POOL_WIDTH] = _pool_out(sums, a, pos, pw_ref[...], ps_ref[...]).astype(yab_ref.dtype)
    tail = ext_ref[tm:tm + HIST_ROWS, :]
    hist_ref[...] = tail
    ext_ref[0:HIST_ROWS, :] = tail

    zb = _dot(xb, w_ref[:, POOL_WIDTH:POOL_WIDTH + 2 * SG_WIDTH])
    y_sg, _ = _spatial_gate(zb, ng_ref[...], nb_ref[...], sgw_ref, sgb_ref[...], SG_CHUNK)
    yab_ref[:, POOL_WIDTH:] = y_sg.astype(yab_ref.dtype)
    _project_qkv(xb, w_ref, q_ref, kt_ref, vt_ref, kb_ref, True)


def _in_prompt_call(x, w_in, pw_bd, pscale, n_g, n_b, sgw, sgb, *, seq, tm):
    n, d = x.shape
    nb = n // seq
    tiles_per_seq = seq // tm
    row = lambda i: (i, 0)
    tok_minor = lambda i: (i // tiles_per_seq, 0, 0, i % tiles_per_seq)
    out_shape = (
        jax.ShapeDtypeStruct((n, POOL_WIDTH + SG_WIDTH), BF16),
        jax.ShapeDtypeStruct((n, MOBA_WIDTH), BF16),
        jax.ShapeDtypeStruct((nb, MOBA_HEADS, HEAD_DIM, seq), F32),
        jax.ShapeDtypeStruct((nb, MOBA_HEADS, HEAD_DIM, seq), F32),
        jax.ShapeDtypeStruct((n, MOBA_WIDTH), BF16),
        jax.ShapeDtypeStruct((nb, HIST_ROWS, POOL_WIDTH), F32),
    )
    out_specs = (
        pl.BlockSpec((tm, POOL_WIDTH + SG_WIDTH), row),
        pl.BlockSpec((tm, MOBA_WIDTH), row),
        pl.BlockSpec((None, MOBA_HEADS, HEAD_DIM, tm), tok_minor),
        pl.BlockSpec((None, MOBA_HEADS, HEAD_DIM, tm), tok_minor),
        pl.BlockSpec((tm, MOBA_WIDTH), row),
        pl.BlockSpec((None, HIST_ROWS, POOL_WIDTH), lambda i: (i // tiles_per_seq, 0, 0)),
    )
    in_specs = [pl.BlockSpec((tm, d), row), _resident(w_in.shape), _resident(pw_bd.shape),
                _resident(pscale.shape), _resident(n_g.shape), _resident(n_b.shape),
                _resident(sgw.shape), _resident(sgb.shape)]
    return pl.pallas_call(
        functools.partial(_in_prompt_kernel, tiles_per_seq=tiles_per_seq),
        grid=(n // tm,), in_specs=in_specs, out_specs=out_specs, out_shape=out_shape,
        scratch_shapes=[pltpu.VMEM((HIST_ROWS + tm, POOL_WIDTH), F32)],
        compiler_params=_params(("arbitrary",)), name="in_prompt",
    )(x, w_in, pw_bd, pscale, n_g, n_b, sgw, sgb)


def _in_sample_kernel(x_ref, hist_in_ref, w_ref, pw_ref, ps_ref, ng_ref, nb_ref, sgw_ref, sgb_ref,
                      yab_ref, q_ref, k_ref, v_ref, hist_ref, sgv_ref, *, dec_seq, pos0):
    n = x_ref.shape[0]
    nb = n // dec_seq
    seg = HIST_ROWS + dec_seq
    xb = x_ref[...].astype(BF16)
    a = _dot(xb, w_ref[:, 0:POOL_WIDTH])
    ext = jnp.concatenate([hist_in_ref[...], a.reshape(nb, dec_seq, POOL_WIDTH)], axis=1)
    sums = _window_sums(ext.reshape(nb * seg, POOL_WIDTH)).reshape(nb, seg, POOL_WIDTH)
    sums = sums[:, HIST_ROWS:, :].reshape(n, POOL_WIDTH)
    pos = pos0 + lax.broadcasted_iota(jnp.int32, a.shape, 0) % dec_seq
    yab_ref[:, 0:POOL_WIDTH] = _pool_out(sums, a, pos, pw_ref[...], ps_ref[...]).astype(yab_ref.dtype)
    hist_ref[...] = ext[:, seg - HIST_ROWS:, :]

    zb = _dot(xb, w_ref[:, POOL_WIDTH:POOL_WIDTH + 2 * SG_WIDTH])
    y_sg, v_n = _spatial_gate(zb, ng_ref[...], nb_ref[...], sgw_ref, sgb_ref[...], n)
    yab_ref[:, POOL_WIDTH:] = y_sg.astype(yab_ref.dtype)
    sgv_ref[...] = v_n
    _project_qkv(xb, w_ref, q_ref, k_ref, v_ref, None, False)


def _in_sample_call(x, hist16, w_in, pw_bd, pscale, n_g, n_b, sgw_bd, sgb_t, *, dec_seq, pos0):
    n, d = x.shape
    nb = n // dec_seq
    out_shape = (
        jax.ShapeDtypeStruct((n, POOL_WIDTH + SG_WIDTH), BF16),
        jax.ShapeDtypeStruct((n, MOBA_WIDTH), F32),
        jax.ShapeDtypeStruct((n, MOBA_WIDTH), F32),
        jax.ShapeDtypeStruct((n, MOBA_WIDTH), F32),
        jax.ShapeDtypeStruct((nb, HIST_ROWS, POOL_WIDTH), F32),
        jax.ShapeDtypeStruct((n, SG_WIDTH), F32),
    )
    args = (x, hist16, w_in, pw_bd, pscale, n_g, n_b, sgw_bd, sgb_t)
    return pl.pallas_call(
        functools.partial(_in_sample_kernel, dec_seq=dec_seq, pos0=pos0),
        grid=(1,), in_specs=[_resident(a.shape) for a in args],
        out_specs=tuple(_resident(s.shape) for s in out_shape), out_shape=out_shape,
        compiler_params=_params(("arbitrary",)), name="in_sample",
    )(*args)


def _block_rank(g, blk, n_valid, n_blocks, axis):
    rank = jnp.zeros(g.shape, jnp.int32)
    for m in range(n_blocks):
        gm = lax.slice_in_dim(g, m, m + 1, axis=axis)
        beats = (gm > g) | ((gm == g) & (m < blk))
        rank = rank + jnp.where(beats & (m < n_valid), 1, 0)
    return rank


GATE_ROWS = 16


def _block_penalty(g, blk, n_valid):
    neg = -jnp.inf
    g = jnp.where(blk < n_valid, g, neg)
    pen = jnp.full(g.shape, MASKED, F32)
    for _ in range(MOBA_TOPK):
        mx = jnp.max(g, axis=0, keepdims=True)
        first = jnp.min(jnp.where(g == mx, blk, GATE_ROWS), axis=0, keepdims=True)
        pick = (blk == first) & (mx > neg)
        pen = jnp.where(pick, 0.0, pen)
        g = jnp.where(pick, neg, g)
    return pen


def _moba_prompt_kernel(q_ref, k_ref, vt_ref, o_ref, kaug_ref, vaug_ref, kmean_ref, qaug_ref, s_ref, smax_ref, m_ref,
                        acc_ref, *, n_blocks):
    blk_rows = MOBA_BLOCK
    qi = pl.program_id(2)
    n_pair = kaug_ref.shape[0]
    n_head = 2 * n_pair

    @pl.when(qi == 0)
    def _():
        lane = lax.broadcasted_iota(jnp.int32, (blk_rows, LANES), 1)
        sub = lax.broadcasted_iota(jnp.int32, (LANES, blk_rows), 0)
        lane1 = lax.broadcasted_iota(jnp.int32, (1, LANES), 1)
        kmean_ref[...] = jnp.zeros(kmean_ref.shape, F32)
        for hp in range(n_pair):
            for n in range(n_blocks):
                kb = k_ref[n * blk_rows:(n + 1) * blk_rows, hp * LANES:(hp + 1) * LANES]
                kaug_ref[hp, n * blk_rows:(n + 1) * blk_rows, 0:LANES] = kb
                kaug_ref[hp, n * blk_rows:(n + 1) * blk_rows, LANES:] = jnp.where(lane == n, 1.0, 0.0).astype(BF16)
                km = jnp.mean(kb.astype(F32), axis=0, keepdims=True)
                kmean_ref[hp, n:n + 1, :] = jnp.where(lane1 < HEAD_DIM, km, 0.0)
                kmean_ref[hp, GATE_ROWS + n:GATE_ROWS + n + 1, :] = jnp.where(lane1 < HEAD_DIM, 0.0, km)
                vt = vt_ref[2 * hp:2 * hp + 2, :, n * blk_rows:(n + 1) * blk_rows].reshape(LANES, blk_rows)
                vaug_ref[n, 2 * hp] = jnp.where(sub < HEAD_DIM, vt, 1.0).astype(BF16)
                vaug_ref[n, 2 * hp + 1] = jnp.where(sub >= HEAD_DIM, vt, 1.0).astype(BF16)

    lane_q = lax.broadcasted_iota(jnp.int32, (blk_rows, LANES), 1)
    key_i = lax.broadcasted_iota(jnp.int32, (blk_rows, blk_rows), 0)
    qry_i = lax.broadcasted_iota(jnp.int32, (blk_rows, blk_rows), 1)
    diag = pl.multiple_of(qi * blk_rows, blk_rows)
    blk = lax.broadcasted_iota(jnp.int32, (GATE_ROWS, blk_rows), 0)
    for hp in range(n_pair):
        q = q_ref[:, hp * LANES:(hp + 1) * LANES]
        kd = kaug_ref[hp, pl.ds(diag, blk_rows), 0:LANES]
        gate_t = _dot_nt(kmean_ref[hp], q.astype(F32), precision=lax.Precision.HIGHEST)
        for h in range(2):
            g = 2 * hp + h
            qh = jnp.where(lane_q // HEAD_DIM == h, q, jnp.zeros_like(q))
            pen_t = _block_penalty(gate_t[h * GATE_ROWS:(h + 1) * GATE_ROWS], blk, qi)
            pen_t = jnp.concatenate([pen_t, jnp.zeros((LANES - GATE_ROWS, blk_rows), F32)], axis=0)
            qaug_ref[g] = jnp.concatenate([qh, pen_t.T.astype(BF16)], axis=1)
            st = jnp.where(key_i <= qry_i, _dot_nt(kd, qh), MASKED)
            m = jnp.max(st, axis=0, keepdims=True)
            m_ref[g] = m
            acc_ref[g] = _dot(vaug_ref[qi, g], jnp.exp(st - m).astype(BF16))

    def scores(t, slot):
        off = pl.multiple_of(t * 2 * blk_rows, 2 * blk_rows)
        for hp in range(n_pair):
            kj = kaug_ref[hp, pl.ds(off, 2 * blk_rows), :]
            for g in (2 * hp, 2 * hp + 1):
                st = _dot_nt(kj, qaug_ref[g])
                s_ref[slot, g] = st
                smax_ref[slot, g] = jnp.max(st, axis=0, keepdims=True)

    def accumulate(t, slot):
        for g in range(n_head):
            st = s_ref[slot, g]
            m_prev = m_ref[g]
            m_new = jnp.maximum(m_prev, smax_ref[slot, g])
            p = jnp.exp(st - m_new).astype(BF16)
            pv = _dot(vaug_ref[2 * t, g], p[0:blk_rows]) + _dot(vaug_ref[2 * t + 1, g], p[blk_rows:])
            acc_ref[g] = jnp.exp(m_prev - m_new) * acc_ref[g] + pv
            m_ref[g] = m_new

    n_pairs = (qi + 1) // 2
    scores(0, 0)

    def body(t, carry):
        scores(t + 1, (t + 1) % 2)
        accumulate(t, t % 2)
        return carry

    lax.fori_loop(0, n_pairs - 1, body, 0)

    @pl.when(n_pairs > 0)
    def _():
        accumulate(n_pairs - 1, (n_pairs - 1) % 2)

    for hp in range(n_pair):
        a0 = acc_ref[2 * hp]
        a1 = acc_ref[2 * hp + 1]
        out_t = jnp.concatenate([a0[0:HEAD_DIM] / a0[HEAD_DIM:], a1[HEAD_DIM:] / a1[0:HEAD_DIM]], axis=0)
        o_ref[:, hp * LANES:(hp + 1) * LANES] = out_t.T.astype(o_ref.dtype)


def _moba_prompt_call(q, kb, vt, *, seq, pairs_per_step):
    n = q.shape[0]
    nb = n // seq
    n_blocks = seq // MOBA_BLOCK
    assert n_blocks <= GATE_ROWS and n_blocks % 2 == 0
    steps = MOBA_WIDTH // (LANES * pairs_per_step)
    width = LANES * pairs_per_step
    heads = 2 * pairs_per_step
    qspec = pl.BlockSpec((MOBA_BLOCK, width), lambda b, hp, i: (b * n_blocks + i, hp))
    kspec = pl.BlockSpec((seq, width), lambda b, hp, i: (b, hp))
    vspec = pl.BlockSpec((None, heads, HEAD_DIM, seq), lambda b, hp, i: (b, hp, 0, 0))
    return pl.pallas_call(
        functools.partial(_moba_prompt_kernel, n_blocks=n_blocks),
        grid=(nb, steps, n_blocks), in_specs=[qspec, kspec, vspec], out_specs=qspec,
        out_shape=jax.ShapeDtypeStruct((n, MOBA_WIDTH), BF16),
        scratch_shapes=[pltpu.VMEM((pairs_per_step, seq, 2 * LANES), BF16),
                        pltpu.VMEM((n_blocks, heads, LANES, MOBA_BLOCK), BF16),
                        pltpu.VMEM((pairs_per_step, 2 * GATE_ROWS, LANES), F32),
                        pltpu.VMEM((heads, MOBA_BLOCK, 2 * LANES), BF16),
                        pltpu.VMEM((2, heads, 2 * MOBA_BLOCK, MOBA_BLOCK), F32),
                        pltpu.VMEM((2, heads, 1, MOBA_BLOCK), F32),
                        pltpu.VMEM((heads, 1, MOBA_BLOCK), F32),
                        pltpu.VMEM((heads, LANES, MOBA_BLOCK), F32)],
        compiler_params=_params(("arbitrary", "arbitrary", "arbitrary")), name="moba_prompt",
    )(q, kb, vt)


def _moba_decode_kernel(pt_ref, q_ref, kn_ref, vn_ref, *rest, pps, n_steps, dec_seq):
    del pt_ref
    k_pages = rest[:pps]
    v_pages = rest[pps:2 * pps]
    o_ref = rest[2 * pps]
    qbd_ref, s_ref, p_ref, sown_ref, pown_ref, gate_ref, l_ref, r_ref = rest[2 * pps + 1:]
    ph = pl.program_id(1)
    st = pl.program_id(2)
    rows = MOBA_HEADS * dec_seq
    n_pages = pps * n_steps
    n_blocks = n_pages // 2
    lane = lax.broadcasted_iota(jnp.int32, (rows, LANES), 1)
    rowi = lax.broadcasted_iota(jnp.int32, (rows, LANES), 0)

    @pl.when((ph == 0) & (st == 0))
    def _():
        qt = jnp.concatenate([q_ref[...]] * MOBA_HEADS, axis=0)
        r2 = lax.broadcasted_iota(jnp.int32, qt.shape, 0)
        c2 = lax.broadcasted_iota(jnp.int32, qt.shape, 1)
        qbd = jnp.where(r2 // dec_seq == c2 // HEAD_DIM, qt, 0.0).astype(BF16)
        qbd_ref[...] = qbd
        gate_ref[...] = jnp.zeros(gate_ref.shape, F32)
        kn = jnp.concatenate([kn_ref[...], jnp.zeros((PAGE_SIZE - dec_seq, MOBA_WIDTH), F32)], axis=0)
        sown_ref[...] = _dot_nt(qbd, kn.astype(BF16))

    @pl.when(ph == 0)
    def _():
        qbd = qbd_ref[...]
        for i in range(0, pps, 2):
            s0 = _dot(qbd, k_pages[i][...].astype(BF16))
            s1 = _dot(qbd, k_pages[i + 1][...].astype(BF16))
            pg = st * pps + i
            s_ref[pg] = s0
            s_ref[pg + 1] = s1
            mean = jnp.sum(s0 + s1, axis=1, keepdims=True) * (1.0 / MOBA_BLOCK)
            gate_ref[...] = jnp.where(lane == pg // 2, mean, gate_ref[...])

    @pl.when((ph == 0) & (st == n_steps - 1))
    def _():
        g = gate_ref[...]
        rank = _block_rank(g, lane, n_blocks, n_blocks, 1)
        keep = (lane < n_blocks) & (rank < MOBA_TOPK)
        own_ok = (lane <= rowi % dec_seq) & (lane < dec_seq)
        s_own = jnp.where(own_ok, sown_ref[...], MASKED)
        m_run = s_own
        for n in range(n_blocks):
            kn = keep[:, n:n + 1]
            for pg in (2 * n, 2 * n + 1):
                m_run = jnp.maximum(m_run, jnp.where(kn, s_ref[pg], MASKED))
        m = jnp.max(m_run, axis=1, keepdims=True)
        p_own = jnp.exp(s_own - m)
        pown_ref[...] = p_own.astype(BF16)
        l_run = p_own
        for n in range(n_blocks):
            kn = keep[:, n:n + 1]
            for pg in (2 * n, 2 * n + 1):
                p = jnp.exp(jnp.where(kn, s_ref[pg], MASKED) - m)
                p_ref[pg] = p.astype(BF16)
                l_run = l_run + p
        l_ref[...] = jnp.sum(l_run, axis=1, keepdims=True)

    @pl.when((ph == 1) & (st == 0))
    def _():
        vn = jnp.concatenate([vn_ref[...], jnp.zeros((PAGE_SIZE - dec_seq, MOBA_WIDTH), F32)], axis=0)
        r_ref[...] = _dot(pown_ref[...], vn.astype(BF16))

    @pl.when(ph == 1)
    def _():
        acc = r_ref[...]
        for i in range(pps):
            acc = acc + _dot_nt(p_ref[st * pps + i], v_pages[i][...].astype(BF16))
        r_ref[...] = acc

    @pl.when((ph == 1) & (st == n_steps - 1))
    def _():
        r = r_ref[...] / l_ref[...]
        c2 = lax.broadcasted_iota(jnp.int32, (dec_seq, MOBA_WIDTH), 1)
        out = jnp.zeros((dec_seq, MOBA_WIDTH), F32)
        for h in range(MOBA_HEADS):
            out = jnp.where(c2 // HEAD_DIM == h, r[h * dec_seq:(h + 1) * dec_seq, :], out)
        o_ref[...] = out


def _moba_decode_call(q, k_new, v_new, cache_k, cache_v, page_table, *, page_base, dec_seq, pps):
    n = q.shape[0]
    nb = n // dec_seq
    n_pages = page_table.shape[1]
    assert n_pages % pps == 0 and pps % 2 == 0 and dec_seq % 8 == 0 and dec_seq <= PAGE_SIZE
    n_steps = n_pages // pps
    n_blocks = n_pages // 2
    assert n_blocks <= LANES
    rows = MOBA_HEADS * dec_seq
    base = page_base

    def k_map(i):
        def f(b, ph, st, pt):
            step = jnp.where(ph == 0, st, n_steps - 1)
            return (base + pt[b, step * pps + i], 0, 0)
        return f

    def v_map(i):
        def f(b, ph, st, pt):
            step = jnp.where(ph == 0, 0, st)
            return (base + pt[b, step * pps + i], 0, 0)
        return f

    tok = pl.BlockSpec((dec_seq, MOBA_WIDTH), lambda b, ph, st, pt: (b, 0))
    page = (None, MOBA_WIDTH, PAGE_SIZE)
    in_specs = ([tok, tok, tok] + [pl.BlockSpec(page, k_map(i)) for i in range(pps)]
                + [pl.BlockSpec(page, v_map(i)) for i in range(pps)])
    grid_spec = pltpu.PrefetchScalarGridSpec(
        num_scalar_prefetch=1, grid=(nb, 2, n_steps), in_specs=in_specs, out_specs=tok,
        scratch_shapes=[pltpu.VMEM((rows, MOBA_WIDTH), BF16),
                        pltpu.VMEM((n_pages, rows, LANES), F32),
                        pltpu.VMEM((n_pages, rows, LANES), BF16),
                        pltpu.VMEM((rows, LANES), F32), pltpu.VMEM((rows, LANES), BF16),
                        pltpu.VMEM((rows, LANES), F32), pltpu.VMEM((rows, 1), F32),
                        pltpu.VMEM((rows, MOBA_WIDTH), F32)])
    return pl.pallas_call(
        functools.partial(_moba_decode_kernel, pps=pps, n_steps=n_steps, dec_seq=dec_seq),
        grid_spec=grid_spec, out_shape=jax.ShapeDtypeStruct((n, MOBA_WIDTH), F32),
        compiler_params=_params(("arbitrary", "arbitrary", "arbitrary")), name="moba_decode",
    )(page_table, q, k_new, v_new, *([cache_k] * pps), *([cache_v] * pps))


def _proj1_kernel(yab_ref, ym_ref, x_ref, wo_ref, g_ref, b_ref, wq_ref, x1_ref, q2_ref, *, alpha, q_scale):
    half = yab_ref.shape[1]
    mix = _dot(yab_ref[...].astype(BF16), wo_ref[0:half, :]) + _dot(ym_ref[...].astype(BF16), wo_ref[half:, :])
    x1 = _layer_norm(alpha * x_ref[...] + mix, g_ref[...], b_ref[...])
    x1_ref[...] = x1
    q2_ref[...] = (_dot(x1.astype(BF16), wq_ref[...]) * q_scale).astype(q2_ref.dtype)


def _proj1_call(yab, ym, x, w_out, g, b, w_q, *, tm, alpha, q_scale, q_dtype):
    n, d = x.shape
    row = lambda i: (i, 0)
    return pl.pallas_call(
        functools.partial(_proj1_kernel, alpha=alpha, q_scale=q_scale),
        grid=(n // tm,),
        in_specs=[pl.BlockSpec((tm, yab.shape[1]), row), pl.BlockSpec((tm, ym.shape[1]), row),
                  pl.BlockSpec((tm, d), row), _resident(w_out.shape), _resident(g.shape), _resident(b.shape),
                  _resident(w_q.shape)],
        out_specs=(pl.BlockSpec((tm, d), row), pl.BlockSpec((tm, d), row)),
        out_shape=(jax.ShapeDtypeStruct((n, d), F32), jax.ShapeDtypeStruct((n, d), q_dtype)),
        compiler_params=_params(("arbitrary",)), name="proj1",
    )(yab, ym, x, w_out, g, b, w_q)


def _xattn_kernel(q_ref, mk_ref, mv_ref, o_ref):
    dh = q_ref.shape[1] // X_HEADS
    for h in range(X_HEADS):
        sl = slice(h * dh, (h + 1) * dh)
        mk, mv = (r[:, h, :] if len(r.shape) == 3 else r[:, sl] for r in (mk_ref, mv_ref))
        s = _dot_nt(q_ref[:, sl].astype(BF16), mk.astype(BF16))
        m = jnp.max(s, axis=1, keepdims=True)
        p = jnp.exp(s - m)
        l = jnp.sum(p, axis=1, keepdims=True)
        o_ref[:, sl] = (_dot(p.astype(BF16), mv.astype(BF16)) / l).astype(o_ref.dtype)


def _xattn_call(q2, mem_k, mem_v, k_index, v_index, *, rows_per_batch, tm):
    n, d = q2.shape
    nb = n // rows_per_batch
    tiles = rows_per_batch // tm
    qspec = pl.BlockSpec((tm, d), lambda b, t: (b * tiles + t, 0))
    if mem_k.ndim == 5:
        mem_spec = lambda g: pl.BlockSpec((None, None) + mem_k.shape[2:], lambda b, t: (g, b, 0, 0, 0))
    else:
        mem_spec = lambda g: pl.BlockSpec((None, mem_k.shape[1] // nb, d), lambda b, t: (g, b, 0))
    return pl.pallas_call(
        _xattn_kernel, grid=(nb, tiles),
        in_specs=[qspec, mem_spec(k_index), mem_spec(v_index)],
        out_specs=qspec, out_shape=jax.ShapeDtypeStruct((n, d), q2.dtype),
        compiler_params=_params(("arbitrary", "arbitrary")), name="xattn",
    )(q2, mem_k, mem_v)


def _proj2_ffn_kernel(o_ref, x1_ref, wo_ref, g2_ref, b2_ref, wg_ref, wu_ref, wd_ref, g3_ref, b3_ref, out_ref,
                      *, alpha, ff_chunk):
    x2 = _layer_norm(alpha * x1_ref[...] + _dot(o_ref[...].astype(BF16), wo_ref[...]), g2_ref[...], b2_ref[...])
    xb = x2.astype(BF16)
    d_ff = wg_ref.shape[1]
    y = jnp.zeros(x2.shape, F32)
    for c in range(d_ff // ff_chunk):
        sl = slice(c * ff_chunk, (c + 1) * ff_chunk)
        gate = _dot(xb, wg_ref[:, sl])
        up = _dot(xb, wu_ref[:, sl])
        hid = gate * (1.0 / (1.0 + jnp.exp(-gate))) * up
        y = y + _dot(hid.astype(BF16), wd_ref[sl, :])
    out_ref[...] = _layer_norm(alpha * x2 + y, g3_ref[...], b3_ref[...])


def _proj2_ffn_call(o, x1, w_o, g2, b2, w_g, w_u, w_d, g3, b3, *, tm, alpha):
    n, d = x1.shape
    row = lambda i: (i, 0)
    consts = (w_o, g2, b2, w_g, w_u, w_d, g3, b3)
    return pl.pallas_call(
        functools.partial(_proj2_ffn_kernel, alpha=alpha, ff_chunk=256),
        grid=(n // tm,),
        in_specs=[pl.BlockSpec((tm, d), row), pl.BlockSpec((tm, d), row)] + [_resident(c.shape) for c in consts],
        out_specs=pl.BlockSpec((tm, d), row), out_shape=jax.ShapeDtypeStruct((n, d), F32),
        compiler_params=_params(("arbitrary",)), name="proj2_ffn",
    )(o, x1, *consts)


def _memkv_kernel(x_ref, w_ref, o_ref):
    o_ref[...] = _dot(x_ref[...].astype(BF16), w_ref[...])


def _memkv_call(mem, w_stack, *, tm):
    n, d = mem.shape
    g = w_stack.shape[0]
    return pl.pallas_call(
        _memkv_kernel, grid=(g, n // tm),
        in_specs=[pl.BlockSpec((tm, d), lambda j, i: (i, 0)), pl.BlockSpec((None, d, d), lambda j, i: (j, 0, 0))],
        out_specs=pl.BlockSpec((None, tm, d), lambda j, i: (j, i, 0)),
        out_shape=jax.ShapeDtypeStruct((g, n, d), F32),
        compiler_params=_params(("arbitrary", "arbitrary")), name="memkv",
    )(mem, w_stack)


def _block_diag(blocks):
    g, c, _ = blocks.shape
    eye = jnp.eye(g, dtype=blocks.dtype)
    return (eye[:, None, :, None] * blocks[:, :, None, :]).reshape(g * c, g * c)


def kernel(x_prompt, x_sample, mem_prompt, state_pool, cache_k, cache_v, cache_mem_k, cache_mem_v, page_table,
           w_in, pool_w, pool_scale, sg_norm_g, sg_norm_b, sg_w, sg_b, w_out, ln1_g, ln1_b, xq_w, xk_w, xv_w,
           xo_w, ln2_g, ln2_b, ffn_gate, ffn_up, ffn_down, ln3_g, ln3_b):
    n_layers = w_in.shape[0]
    bp, seq, d = x_prompt.shape
    bs, dec_seq, _ = x_sample.shape
    n_mem = mem_prompt.shape[1]
    n_phys = cache_k.shape[1]
    past_len = page_table.shape[1] * PAGE_SIZE
    assert past_len % MOBA_BLOCK == 0 and dec_seq <= MOBA_BLOCK and seq % ROW_TILE == 0
    alpha = (2 * n_layers) ** 0.25
    q_scale = (d // X_HEADS) ** -0.5

    xp = x_prompt.reshape(bp * seq, d)
    xs = x_sample.reshape(bs * dec_seq, d)
    paged = lambda c: jnp.transpose(c, (0, 1, 3, 4, 2)).reshape(n_layers * n_phys, MOBA_WIDTH, PAGE_SIZE)
    cache_k2, cache_v2 = paged(cache_k), paged(cache_v)
    hist16 = jnp.pad(state_pool, ((0, 0), (0, 0), (1, 0), (0, 0)))

    vec = lambda p: p.reshape(n_layers, 1, -1)
    w_in_b, w_out_b, xq_b, xo_b = (w.astype(BF16) for w in (w_in, w_out, xq_w, xo_w))
    wg_b, wu_b, wd_b = (w.astype(BF16) for w in (ffn_gate, ffn_up, ffn_down))
    pw_bd = jnp.stack([_block_diag(pool_w[l]) for l in range(n_layers)]).astype(BF16)
    causal = jnp.tril(jnp.ones((SG_CHUNK, SG_CHUNK), bool))
    sgw_p = jnp.where(causal, sg_w, 0.0).astype(BF16)
    sgb_p = jnp.repeat(jnp.swapaxes(sg_b, 1, 2), HEAD_DIM, axis=2)
    eye_s = jnp.eye(bs, dtype=F32)
    w_dec = jnp.where(causal[:dec_seq, :dec_seq], sg_w[:, :, :dec_seq, :dec_seq], 0.0)
    sgw_s = (eye_s[None, None, :, None, :, None] * w_dec[:, :, None, :, None, :]).reshape(
        n_layers, SG_HEADS, bs * dec_seq, bs * dec_seq).astype(BF16)
    sgb_s = jnp.tile(sgb_p[:, :dec_seq, :], (1, bs, 1))
    kv_stack = jnp.concatenate([xk_w, xv_w], axis=0).astype(BF16)

    memkv = _memkv_call(mem_prompt.reshape(bp * n_mem, d), kv_stack, tm=ROW_TILE)

    pool_p, pool_s, kp_l, vp_l, ks_l, vs_l, sgv_l = [], [], [], [], [], [], []
    for l in range(n_layers):
        ln = lambda p: vec(p)[l]
        yab, q, kt, vt, kb, hist = _in_prompt_call(
            xp, w_in_b[l], pw_bd[l], ln(pool_scale), ln(sg_norm_g), ln(sg_norm_b), sgw_p[l], sgb_p[l],
            seq=seq, tm=ROW_TILE)
        ym = _moba_prompt_call(q, kb, vt, seq=seq, pairs_per_step=MOBA_PAIRS_PER_STEP)
        x1, q2 = _proj1_call(yab, ym, xp, w_out_b[l], ln(ln1_g), ln(ln1_b), xq_b[l], tm=ROW_TILE, alpha=alpha,
                             q_scale=q_scale, q_dtype=BF16)
        o = _xattn_call(q2, memkv, memkv, l, n_layers + l, rows_per_batch=seq, tm=ROW_TILE)
        xp = _proj2_ffn_call(o, x1, xo_b[l], ln(ln2_g), ln(ln2_b), wg_b[l], wu_b[l], wd_b[l], ln(ln3_g),
                             ln(ln3_b), tm=ROW_TILE, alpha=alpha)
        pool_p.append(hist[:, 1:, :]); kp_l.append(kt); vp_l.append(vt)
        yab, q, k, v, hist, sgv = _in_sample_call(
            xs, hist16[l], w_in_b[l], pw_bd[l], ln(pool_scale), ln(sg_norm_g), ln(sg_norm_b), sgw_s[l], sgb_s[l],
            dec_seq=dec_seq, pos0=past_len)
        ym = _moba_decode_call(q, k, v, cache_k2, cache_v2, page_table, page_base=l * n_phys, dec_seq=dec_seq,
                               pps=DECODE_PAGES_PER_STEP)
        x1, q2 = _proj1_call(yab, ym, xs, w_out_b[l], ln(ln1_g), ln(ln1_b), xq_b[l], tm=bs * dec_seq, alpha=alpha,
                             q_scale=q_scale, q_dtype=F32)
        o = _xattn_call(q2, cache_mem_k, cache_mem_v, l, l, rows_per_batch=dec_seq, tm=dec_seq)
        xs = _proj2_ffn_call(o, x1, xo_b[l], ln(ln2_g), ln(ln2_b), wg_b[l], wu_b[l], wd_b[l], ln(ln3_g),
                             ln(ln3_b), tm=bs * dec_seq, alpha=alpha)
        pool_s.append(hist[:, 1:, :]); ks_l.append(k); vs_l.append(v); sgv_l.append(sgv)

    heads = lambda a, nb_, s_: a.reshape(nb_, s_, MOBA_HEADS, HEAD_DIM)
    tok_major = lambda a: jnp.transpose(a, (0, 1, 4, 2, 3))
    mem_shape = (n_layers, bp, n_mem, X_HEADS, d // X_HEADS)
    return (xp.reshape(bp, seq, d), xs.reshape(bs, dec_seq, d),
            jnp.stack(pool_p), jnp.stack(pool_s),
            tok_major(jnp.stack(kp_l)), tok_major(jnp.stack(vp_l)),
            jnp.stack([heads(a, bs, dec_seq) for a in ks_l]), jnp.stack([heads(a, bs, dec_seq) for a in vs_l]),
            jnp.stack([a.reshape(bs, dec_seq, SG_WIDTH) for a in sgv_l]),
            memkv[:n_layers].reshape(mem_shape), memkv[n_layers:].reshape(mem_shape))
```

```python
import functools

import jax
import jax.numpy as jnp
from jax import lax
from jax.experimental import pallas as pl
from jax.experimental.pallas import tpu as pltpu

F32 = jnp.float32
BF16 = jnp.bfloat16

HEAD_DIM = 64
POOL_WIDTH = 256
POOL_WINDOWS = (2, 4, 8, 16)
POOL_GROUP_DIM = POOL_WIDTH // len(POOL_WINDOWS)
POOL_HIST = max(POOL_WINDOWS) - 1
HIST_ROWS = POOL_HIST + 1
SG_WIDTH = 256
SG_HEADS = SG_WIDTH // HEAD_DIM
SG_CHUNK = 128
MOBA_WIDTH = 512
MOBA_HEADS = MOBA_WIDTH // HEAD_DIM
MOBA_BLOCK = 256
MOBA_TOPK = 3
PAGE_SIZE = 128
X_HEADS = 4
LN_EPS = 1e-5
MASKED = -1e30

LANES = 128
VMEM_LIMIT_BYTES = 56 * 1024 * 1024
ROW_TILE = 512
DECODE_PAGES_PER_STEP = 32
IN_ROW_TILE = 1024
MOBA_PAIRS_PER_STEP = 2


def _dot(a, b):
    return jnp.dot(a, b, preferred_element_type=F32)


def _dot_nt(a, b, precision=None):
    return lax.dot_general(a, b, (((1,), (1,)), ((), ())), preferred_element_type=F32, precision=precision)


def _layer_norm(y, g, b):
    mu = jnp.mean(y, axis=-1, keepdims=True)
    d = y - mu
    var = jnp.mean(d * d, axis=-1, keepdims=True)
    return d * lax.rsqrt(var + LN_EPS) * g + b


def _params(sem):
    return pltpu.CompilerParams(dimension_semantics=sem, vmem_limit_bytes=VMEM_LIMIT_BYTES)


def _resident(shape):
    nd = len(shape)
    return pl.BlockSpec(shape, lambda *_: (0,) * nd, pipeline_mode=pl.Buffered(1))


def _window_sums(ext):
    s2 = ext + pltpu.roll(ext, 1, 0)
    s4 = s2 + pltpu.roll(s2, 2, 0)
    s8 = s4 + pltpu.roll(s4, 4, 0)
    s16 = s8 + pltpu.roll(s8, 8, 0)
    lane = lax.broadcasted_iota(jnp.int32, ext.shape, 1)
    g = POOL_GROUP_DIM
    return jnp.where(lane < g, s2, jnp.where(lane < 2 * g, s4, jnp.where(lane < 3 * g, s8, s16)))


def _pool_out(sums, a, pos, pw_bd, scale):
    lane = lax.broadcasted_iota(jnp.int32, a.shape, 1)
    g = POOL_GROUP_DIM
    win = jnp.where(lane < g, 2, jnp.where(lane < 2 * g, 4, jnp.where(lane < 3 * g, 8, 16)))
    cnt = jnp.minimum(win, pos + 1).astype(F32)
    pooled = sums / cnt - a
    return _dot(pooled.astype(BF16), pw_bd) * scale


def _spatial_gate(zb, n_g, n_b, sgw_ref, bias, chunk):
    z = jax.nn.gelu(zb, approximate=True)
    u = z[:, :SG_WIDTH]
    v = _layer_norm(z[:, SG_WIDTH:], n_g, n_b)
    vb = v.astype(BF16)
    lane = lax.broadcasted_iota(jnp.int32, (chunk, LANES), 1)
    rows = []
    for c in range(zb.shape[0] // chunk):
        vc = vb[c * chunk:(c + 1) * chunk]
        cols = []
        for pair in range(SG_HEADS // 2):
            vp = vc[:, pair * LANES:(pair + 1) * LANES]
            s0 = _dot(sgw_ref[2 * pair], vp)
            s1 = _dot(sgw_ref[2 * pair + 1], vp)
            cols.append(jnp.where(lane < HEAD_DIM, s0, s1))
        rows.append(jnp.concatenate(cols, axis=1) + bias)
    s = rows[0] if len(rows) == 1 else jnp.concatenate(rows, axis=0)
    return u * s, v


def _project_qkv(xb, w_ref, q_ref, k_ref, v_ref, kb_ref, transposed):
    c0 = POOL_WIDTH + 2 * SG_WIDTH
    q = _dot(xb, w_ref[:, c0:c0 + MOBA_WIDTH]) * (HEAD_DIM ** -0.5)
    q_ref[...] = q.astype(q_ref.dtype)
    k = _dot(xb, w_ref[:, c0 + MOBA_WIDTH:c0 + 2 * MOBA_WIDTH])
    v = _dot(xb, w_ref[:, c0 + 2 * MOBA_WIDTH:c0 + 3 * MOBA_WIDTH])
    if transposed:
        k_ref[...] = k.T.reshape(k_ref.shape)
        v_ref[...] = v.T.reshape(v_ref.shape)
        kb_ref[...] = k.astype(BF16)
    else:
        k_ref[...] = k
        v_ref[...] = v


def _in_prompt_kernel(x_ref, w_ref, pw_ref, ps_ref, ng_ref, nb_ref, sgw_ref, sgb_ref,
                      yab_ref, q_ref, kt_ref, vt_ref, kb_ref, hist_ref, ext_ref, *, tiles_per_seq):
    tm = x_ref.shape[0]
    t = pl.program_id(0) % tiles_per_seq

    @pl.when(t == 0)
    def _():
        ext_ref[0:HIST_ROWS, :] = jnp.zeros((HIST_ROWS, POOL_WIDTH), F32)

    xb = x_ref[...].astype(BF16)
    a = _dot(xb, w_ref[:, 0:POOL_WIDTH])
    ext_ref[HIST_ROWS:, :] = a
    sums = _window_sums(ext_ref[...])[HIST_ROWS:]
    pos = t * tm + lax.broadcasted_iota(jnp.int32, a.shape, 0)
    yab_ref[:, 0:POOL_WIDTH] = _pool_out(sums, a, pos, pw_ref[...], ps_ref[...]).astype(yab_ref.dtype)
    tail = ext_ref[tm:tm + HIST_ROWS, :]
    hist_ref[...] = tail
    ext_ref[0:HIST_ROWS, :] = tail

    zb = _dot(xb, w_ref[:, POOL_WIDTH:POOL_WIDTH + 2 * SG_WIDTH])
    y_sg, _ = _spatial_gate(zb, ng_ref[...], nb_ref[...], sgw_ref, sgb_ref[...], SG_CHUNK)
    yab_ref[:, POOL_WIDTH:] = y_sg.astype(yab_ref.dtype)
    _project_qkv(xb, w_ref, q_ref, kt_ref, vt_ref, kb_ref, True)


def _in_prompt_call(x, w_in, pw_bd, pscale, n_g, n_b, sgw, sgb, *, seq, tm):
    n, d = x.shape
    nb = n // seq
    tiles_per_seq = seq // tm
    row = lambda i: (i, 0)
    tok_minor = lambda i: (i // tiles_per_seq, 0, 0, i % tiles_per_seq)
    out_shape = (
        jax.ShapeDtypeStruct((n, POOL_WIDTH + SG_WIDTH), BF16),
        jax.ShapeDtypeStruct((n, MOBA_WIDTH), BF16),
        jax.ShapeDtypeStruct((nb, MOBA_HEADS, HEAD_DIM, seq), F32),
        jax.ShapeDtypeStruct((nb, MOBA_HEADS, HEAD_DIM, seq), F32),
        jax.ShapeDtypeStruct((n, MOBA_WIDTH), BF16),
        jax.ShapeDtypeStruct((nb, HIST_ROWS, POOL_WIDTH), F32),
    )
    out_specs = (
        pl.BlockSpec((tm, POOL_WIDTH + SG_WIDTH), row),
        pl.BlockSpec((tm, MOBA_WIDTH), row),
        pl.BlockSpec((None, MOBA_HEADS, HEAD_DIM, tm), tok_minor),
        pl.BlockSpec((None, MOBA_HEADS, HEAD_DIM, tm), tok_minor),
        pl.BlockSpec((tm, MOBA_WIDTH), row),
        pl.BlockSpec((None, HIST_ROWS, POOL_WIDTH), lambda i: (i // tiles_per_seq, 0, 0)),
    )
    in_specs = [pl.BlockSpec((tm, d), row), _resident(w_in.shape), _resident(pw_bd.shape),
                _resident(pscale.shape), _resident(n_g.shape), _resident(n_b.shape),
                _resident(sgw.shape), _resident(sgb.shape)]
    return pl.pallas_call(
        functools.partial(_in_prompt_kernel, tiles_per_seq=tiles_per_seq),
        grid=(n // tm,), in_specs=in_specs, out_specs=out_specs, out_shape=out_shape,
        scratch_shapes=[pltpu.VMEM((HIST_ROWS + tm, POOL_WIDTH), F32)],
        compiler_params=_params(("arbitrary",)), name="in_prompt",
    )(x, w_in, pw_bd, pscale, n_g, n_b, sgw, sgb)


def _in_sample_kernel(x_ref, hist_in_ref, w_ref, pw_ref, ps_ref, ng_ref, nb_ref, sgw_ref, sgb_ref,
                      yab_ref, q_ref, k_ref, v_ref, hist_ref, sgv_ref, *, dec_seq, pos0):
    n = x_ref.shape[0]
    nb = n // dec_seq
    seg = HIST_ROWS + dec_seq
    xb = x_ref[...].astype(BF16)
    a = _dot(xb, w_ref[:, 0:POOL_WIDTH])
    ext = jnp.concatenate([hist_in_ref[...], a.reshape(nb, dec_seq, POOL_WIDTH)], axis=1)
    sums = _window_sums(ext.reshape(nb * seg, POOL_WIDTH)).reshape(nb, seg, POOL_WIDTH)
    sums = sums[:, HIST_ROWS:, :].reshape(n, POOL_WIDTH)
    pos = pos0 + lax.broadcasted_iota(jnp.int32, a.shape, 0) % dec_seq
    yab_ref[:, 0:POOL_WIDTH] = _pool_out(sums, a, pos, pw_ref[...], ps_ref[...]).astype(yab_ref.dtype)
    hist_ref[...] = ext[:, seg - HIST_ROWS:, :]

    zb = _dot(xb, w_ref[:, POOL_WIDTH:POOL_WIDTH + 2 * SG_WIDTH])
    y_sg, v_n = _spatial_gate(zb, ng_ref[...], nb_ref[...], sgw_ref, sgb_ref[...], n)
    yab_ref[:, POOL_WIDTH:] = y_sg.astype(yab_ref.dtype)
    sgv_ref[...] = v_n
    _project_qkv(xb, w_ref, q_ref, k_ref, v_ref, None, False)


def _in_sample_call(x, hist16, w_in, pw_bd, pscale, n_g, n_b, sgw_bd, sgb_t, *, dec_seq, pos0):
    n, d = x.shape
    nb = n // dec_seq
    out_shape = (
        jax.ShapeDtypeStruct((n, POOL_WIDTH + SG_WIDTH), BF16),
        jax.ShapeDtypeStruct((n, MOBA_WIDTH), F32),
        jax.ShapeDtypeStruct((n, MOBA_WIDTH), F32),
        jax.ShapeDtypeStruct((n, MOBA_WIDTH), F32),
        jax.ShapeDtypeStruct((nb, HIST_ROWS, POOL_WIDTH), F32),
        jax.ShapeDtypeStruct((n, SG_WIDTH), F32),
    )
    args = (x, hist16, w_in, pw_bd, pscale, n_g, n_b, sgw_bd, sgb_t)
    return pl.pallas_call(
        functools.partial(_in_sample_kernel, dec_seq=dec_seq, pos0=pos0),
        grid=(1,), in_specs=[_resident(a.shape) for a in args],
        out_specs=tuple(_resident(s.shape) for s in out_shape), out_shape=out_shape,
        compiler_params=_params(("arbitrary",)), name="in_sample",
    )(*args)


def _block_rank(g, blk, n_valid, n_blocks, axis):
    rank = jnp.zeros(g.shape, jnp.int32)
    for m in range(n_blocks):
        gm = lax.slice_in_dim(g, m, m + 1, axis=axis)
        beats = (gm > g) | ((gm == g) & (m < blk))
        rank = rank + jnp.where(beats & (m < n_valid), 1, 0)
    return rank


GATE_ROWS = 16


def _block_penalty(g, blk, n_valid):
    neg = -jnp.inf
    g = jnp.where(blk < n_valid, g, neg)
    pen = jnp.full(g.shape, MASKED, F32)
    for _ in range(MOBA_TOPK):
        mx = jnp.max(g, axis=0, keepdims=True)
        first = jnp.min(jnp.where(g == mx, blk, GATE_ROWS), axis=0, keepdims=True)
        pick = (blk == first) & (mx > neg)
        pen = jnp.where(pick, 0.0, pen)
        g = jnp.where(pick, neg, g)
    return pen


def _moba_prompt_kernel(q_ref, k_ref, vt_ref, o_ref, kaug_ref, vaug_ref, kmean_ref, qaug_ref, s_ref, smax_ref, m_ref,
                        acc_ref, *, n_blocks):
    blk_rows = MOBA_BLOCK
    qi = pl.program_id(2)
    n_pair = kaug_ref.shape[0]
    n_head = 2 * n_pair

    @pl.when(qi == 0)
    def _():
        lane = lax.broadcasted_iota(jnp.int32, (blk_rows, LANES), 1)
        sub = lax.broadcasted_iota(jnp.int32, (LANES, blk_rows), 0)
        lane1 = lax.broadcasted_iota(jnp.int32, (1, LANES), 1)
        kmean_ref[...] = jnp.zeros(kmean_ref.shape, F32)
        for hp in range(n_pair):
            for n in range(n_blocks):
                kb = k_ref[n * blk_rows:(n + 1) * blk_rows, hp * LANES:(hp + 1) * LANES]
                kaug_ref[hp, n * blk_rows:(n + 1) * blk_rows, 0:LANES] = kb
                kaug_ref[hp, n * blk_rows:(n + 1) * blk_rows, LANES:] = jnp.where(lane == n, 1.0, 0.0).astype(BF16)
                km = jnp.mean(kb.astype(F32), axis=0, keepdims=True)
                kmean_ref[hp, n:n + 1, :] = jnp.where(lane1 < HEAD_DIM, km, 0.0)
                kmean_ref[hp, GATE_ROWS + n:GATE_ROWS + n + 1, :] = jnp.where(lane1 < HEAD_DIM, 0.0, km)
                vt = vt_ref[2 * hp:2 * hp + 2, :, n * blk_rows:(n + 1) * blk_rows].reshape(LANES, blk_rows)
                vaug_ref[n, 2 * hp] = jnp.where(sub < HEAD_DIM, vt, 1.0).astype(BF16)
                vaug_ref[n, 2 * hp + 1] = jnp.where(sub >= HEAD_DIM, vt, 1.0).astype(BF16)

    lane_q = lax.broadcasted_iota(jnp.int32, (blk_rows, LANES), 1)
    blk = lax.broadcasted_iota(jnp.int32, (GATE_ROWS, blk_rows), 0)
    for hp in range(n_pair):
        q = q_ref[:, hp * LANES:(hp + 1) * LANES]
        gate_t = _dot_nt(kmean_ref[hp], q.astype(F32), precision=lax.Precision.HIGHEST)
        for h in range(2):
            qh = jnp.where(lane_q // HEAD_DIM == h, q, jnp.zeros_like(q))
            pen_t = _block_penalty(gate_t[h * GATE_ROWS:(h + 1) * GATE_ROWS], blk, qi)
            pen_t = jnp.where(blk == qi, 0.0, pen_t)
            pen_t = jnp.concatenate([pen_t, jnp.zeros((LANES - GATE_ROWS, blk_rows), F32)], axis=0)
            qaug_ref[2 * hp + h] = jnp.concatenate([qh, pen_t.T.astype(BF16)], axis=1)

    def scores(t, slot):
        off = pl.multiple_of(t * 2 * blk_rows, 2 * blk_rows)
        for hp in range(n_pair):
            kj = kaug_ref[hp, pl.ds(off, 2 * blk_rows), :]
            for g in (2 * hp, 2 * hp + 1):
                st = _dot_nt(kj, qaug_ref[g])
                s_ref[slot, g] = st
                smax_ref[slot, g] = jnp.max(st, axis=0, keepdims=True)

    def accumulate(t, slot):
        for g in range(n_head):
            st = s_ref[slot, g]
            m_prev = m_ref[g]
            m_new = jnp.maximum(m_prev, smax_ref[slot, g])
            p = jnp.exp(st - m_new).astype(BF16)
            pv = _dot(vaug_ref[2 * t, g], p[0:blk_rows]) + _dot(vaug_ref[2 * t + 1, g], p[blk_rows:])
            acc_ref[g] = jnp.exp(m_prev - m_new) * acc_ref[g] + pv
            m_ref[g] = m_new

    own = qi // 2
    scores(own, 0)
    scores(jnp.maximum(own - 1, 0), 1)
    key_i = lax.broadcasted_iota(jnp.int32, (2 * blk_rows, blk_rows), 0) - (qi % 2) * blk_rows
    qry_i = lax.broadcasted_iota(jnp.int32, (2 * blk_rows, blk_rows), 1)
    for g in range(n_head):
        st = jnp.where(key_i <= qry_i, s_ref[0, g], MASKED)
        m = jnp.max(st, axis=0, keepdims=True)
        p = jnp.exp(st - m).astype(BF16)
        acc_ref[g] = _dot(vaug_ref[2 * own, g], p[0:blk_rows]) + _dot(vaug_ref[2 * own + 1, g], p[blk_rows:])
        m_ref[g] = m

    def body(j, carry):
        scores(own - 2 - j, j % 2)
        accumulate(own - 1 - j, (1 + j) % 2)
        return carry

    lax.fori_loop(0, own - 1, body, 0)

    @pl.when(own > 0)
    def _():
        accumulate(0, own % 2)

    for hp in range(n_pair):
        a0 = acc_ref[2 * hp]
        a1 = acc_ref[2 * hp + 1]
        out_t = jnp.concatenate([a0[0:HEAD_DIM] / a0[HEAD_DIM:], a1[HEAD_DIM:] / a1[0:HEAD_DIM]], axis=0)
        o_ref[:, hp * LANES:(hp + 1) * LANES] = out_t.T.astype(o_ref.dtype)


def _moba_prompt_call(q, kb, vt, *, seq, pairs_per_step):
    n = q.shape[0]
    nb = n // seq
    n_blocks = seq // MOBA_BLOCK
    assert n_blocks <= GATE_ROWS and n_blocks % 2 == 0
    steps = MOBA_WIDTH // (LANES * pairs_per_step)
    width = LANES * pairs_per_step
    heads = 2 * pairs_per_step
    qspec = pl.BlockSpec((MOBA_BLOCK, width), lambda b, hp, i: (b * n_blocks + i, hp))
    kspec = pl.BlockSpec((seq, width), lambda b, hp, i: (b, hp))
    vspec = pl.BlockSpec((None, heads, HEAD_DIM, seq), lambda b, hp, i: (b, hp, 0, 0))
    return pl.pallas_call(
        functools.partial(_moba_prompt_kernel, n_blocks=n_blocks),
        grid=(nb, steps, n_blocks), in_specs=[qspec, kspec, vspec], out_specs=qspec,
        out_shape=jax.ShapeDtypeStruct((n, MOBA_WIDTH), BF16),
        scratch_shapes=[pltpu.VMEM((pairs_per_step, seq, 2 * LANES), BF16),
                        pltpu.VMEM((n_blocks, heads, LANES, MOBA_BLOCK), BF16),
                        pltpu.VMEM((pairs_per_step, 2 * GATE_ROWS, LANES), F32),
                        pltpu.VMEM((heads, MOBA_BLOCK, 2 * LANES), BF16),
                        pltpu.VMEM((2, heads, 2 * MOBA_BLOCK, MOBA_BLOCK), F32),
                        pltpu.VMEM((2, heads, 1, MOBA_BLOCK), F32),
                        pltpu.VMEM((heads, 1, MOBA_BLOCK), F32),
                        pltpu.VMEM((heads, LANES, MOBA_BLOCK), F32)],
        compiler_params=_params(("arbitrary", "arbitrary", "arbitrary")), name="moba_prompt",
    )(q, kb, vt)


def _moba_decode_kernel(pt_ref, q_ref, kn_ref, vn_ref, *rest, pps, n_steps, dec_seq):
    del pt_ref
    k_pages = rest[:pps]
    v_pages = rest[pps:2 * pps]
    o_ref = rest[2 * pps]
    qbd_ref, s_ref, p_ref, sown_ref, pown_ref, gate_ref, l_ref, r_ref = rest[2 * pps + 1:]
    ph = pl.program_id(1)
    st = pl.program_id(2)
    rows = MOBA_HEADS * dec_seq
    n_pages = pps * n_steps
    n_blocks = n_pages // 2
    lane = lax.broadcasted_iota(jnp.int32, (rows, LANES), 1)
    rowi = lax.broadcasted_iota(jnp.int32, (rows, LANES), 0)

    @pl.when((ph == 0) & (st == 0))
    def _():
        qt = jnp.concatenate([q_ref[...]] * MOBA_HEADS, axis=0)
        r2 = lax.broadcasted_iota(jnp.int32, qt.shape, 0)
        c2 = lax.broadcasted_iota(jnp.int32, qt.shape, 1)
        qbd = jnp.where(r2 // dec_seq == c2 // HEAD_DIM, qt, 0.0).astype(BF16)
        qbd_ref[...] = qbd
        gate_ref[...] = jnp.zeros(gate_ref.shape, F32)
        kn = jnp.concatenate([kn_ref[...], jnp.zeros((PAGE_SIZE - dec_seq, MOBA_WIDTH), F32)], axis=0)
        sown_ref[...] = _dot_nt(qbd, kn.astype(BF16))

    @pl.when(ph == 0)
    def _():
        qbd = qbd_ref[...]
        for i in range(0, pps, 2):
            s0 = _dot(qbd, k_pages[i][...].astype(BF16))
            s1 = _dot(qbd, k_pages[i + 1][...].astype(BF16))
            pg = st * pps + i
            s_ref[pg] = s0
            s_ref[pg + 1] = s1
            mean = jnp.sum(s0 + s1, axis=1, keepdims=True) * (1.0 / MOBA_BLOCK)
            gate_ref[...] = jnp.where(lane == pg // 2, mean, gate_ref[...])

    @pl.when((ph == 0) & (st == n_steps - 1))
    def _():
        g = gate_ref[...]
        rank = _block_rank(g, lane, n_blocks, n_blocks, 1)
        keep = (lane < n_blocks) & (rank < MOBA_TOPK)
        own_ok = (lane <= rowi % dec_seq) & (lane < dec_seq)
        s_own = jnp.where(own_ok, sown_ref[...], MASKED)
        m_run = s_own
        for n in range(n_blocks):
            kn = keep[:, n:n + 1]
            for pg in (2 * n, 2 * n + 1):
                m_run = jnp.maximum(m_run, jnp.where(kn, s_ref[pg], MASKED))
        m = jnp.max(m_run, axis=1, keepdims=True)
        p_own = jnp.exp(s_own - m)
        pown_ref[...] = p_own.astype(BF16)
        l_run = p_own
        for n in range(n_blocks):
            kn = keep[:, n:n + 1]
            for pg in (2 * n, 2 * n + 1):
                p = jnp.exp(jnp.where(kn, s_ref[pg], MASKED) - m)
                p_ref[pg] = p.astype(BF16)
                l_run = l_run + p
        l_ref[...] = jnp.sum(l_run, axis=1, keepdims=True)

    @pl.when((ph == 1) & (st == 0))
    def _():
        vn = jnp.concatenate([vn_ref[...], jnp.zeros((PAGE_SIZE - dec_seq, MOBA_WIDTH), F32)], axis=0)
        r_ref[...] = _dot(pown_ref[...], vn.astype(BF16))

    @pl.when(ph == 1)
    def _():
        acc = r_ref[...]
        for i in range(pps):
            acc = acc + _dot_nt(p_ref[st * pps + i], v_pages[i][...].astype(BF16))
        r_ref[...] = acc

    @pl.when((ph == 1) & (st == n_steps - 1))
    def _():
        r = r_ref[...] / l_ref[...]
        c2 = lax.broadcasted_iota(jnp.int32, (dec_seq, MOBA_WIDTH), 1)
        out = jnp.zeros((dec_seq, MOBA_WIDTH), F32)
        for h in range(MOBA_HEADS):
            out = jnp.where(c2 // HEAD_DIM == h, r[h * dec_seq:(h + 1) * dec_seq, :], out)
        o_ref[...] = out


def _moba_decode_call(q, k_new, v_new, cache_k, cache_v, page_table, *, page_base, dec_seq, pps):
    n = q.shape[0]
    nb = n // dec_seq
    n_pages = page_table.shape[1]
    assert n_pages % pps == 0 and pps % 2 == 0 and dec_seq % 8 == 0 and dec_seq <= PAGE_SIZE
    n_steps = n_pages // pps
    n_blocks = n_pages // 2
    assert n_blocks <= LANES
    rows = MOBA_HEADS * dec_seq
    base = page_base

    def k_map(i):
        def f(b, ph, st, pt):
            step = jnp.where(ph == 0, st, n_steps - 1)
            return (base + pt[b, step * pps + i], 0, 0)
        return f

    def v_map(i):
        def f(b, ph, st, pt):
            step = jnp.where(ph == 0, 0, st)
            return (base + pt[b, step * pps + i], 0, 0)
        return f

    tok = pl.BlockSpec((dec_seq, MOBA_WIDTH), lambda b, ph, st, pt: (b, 0))
    page = (None, MOBA_WIDTH, PAGE_SIZE)
    in_specs = ([tok, tok, tok] + [pl.BlockSpec(page, k_map(i)) for i in range(pps)]
                + [pl.BlockSpec(page, v_map(i)) for i in range(pps)])
    grid_spec = pltpu.PrefetchScalarGridSpec(
        num_scalar_prefetch=1, grid=(nb, 2, n_steps), in_specs=in_specs, out_specs=tok,
        scratch_shapes=[pltpu.VMEM((rows, MOBA_WIDTH), BF16),
                        pltpu.VMEM((n_pages, rows, LANES), F32),
                        pltpu.VMEM((n_pages, rows, LANES), BF16),
                        pltpu.VMEM((rows, LANES), F32), pltpu.VMEM((rows, LANES), BF16),
                        pltpu.VMEM((rows, LANES), F32), pltpu.VMEM((rows, 1), F32),
                        pltpu.VMEM((rows, MOBA_WIDTH), F32)])
    return pl.pallas_call(
        functools.partial(_moba_decode_kernel, pps=pps, n_steps=n_steps, dec_seq=dec_seq),
        grid_spec=grid_spec, out_shape=jax.ShapeDtypeStruct((n, MOBA_WIDTH), F32),
        compiler_params=_params(("arbitrary", "arbitrary", "arbitrary")), name="moba_decode",
    )(page_table, q, k_new, v_new, *([cache_k] * pps), *([cache_v] * pps))


def _proj1_kernel(yab_ref, ym_ref, x_ref, wo_ref, g_ref, b_ref, wq_ref, x1_ref, q2_ref, *, alpha, q_scale):
    half = yab_ref.shape[1]
    mix = _dot(yab_ref[...].astype(BF16), wo_ref[0:half, :]) + _dot(ym_ref[...].astype(BF16), wo_ref[half:, :])
    x1 = _layer_norm(alpha * x_ref[...] + mix, g_ref[...], b_ref[...])
    x1_ref[...] = x1
    q2_ref[...] = (_dot(x1.astype(BF16), wq_ref[...]) * q_scale).astype(q2_ref.dtype)


def _proj1_call(yab, ym, x, w_out, g, b, w_q, *, tm, alpha, q_scale, q_dtype):
    n, d = x.shape
    row = lambda i: (i, 0)
    return pl.pallas_call(
        functools.partial(_proj1_kernel, alpha=alpha, q_scale=q_scale),
        grid=(n // tm,),
        in_specs=[pl.BlockSpec((tm, yab.shape[1]), row), pl.BlockSpec((tm, ym.shape[1]), row),
                  pl.BlockSpec((tm, d), row), _resident(w_out.shape), _resident(g.shape), _resident(b.shape),
                  _resident(w_q.shape)],
        out_specs=(pl.BlockSpec((tm, d), row), pl.BlockSpec((tm, d), row)),
        out_shape=(jax.ShapeDtypeStruct((n, d), F32), jax.ShapeDtypeStruct((n, d), q_dtype)),
        compiler_params=_params(("arbitrary",)), name="proj1",
    )(yab, ym, x, w_out, g, b, w_q)


def _xattn_kernel(q_ref, mk_ref, mv_ref, o_ref, *, heads):
    dh = q_ref.shape[1] // heads
    for h in range(heads):
        sl = slice(h * dh, (h + 1) * dh)
        s = _dot_nt(q_ref[:, sl].astype(BF16), mk_ref[:, sl].astype(BF16))
        m = jnp.max(s, axis=1, keepdims=True)
        p = jnp.exp(s - m)
        l = jnp.sum(p, axis=1, keepdims=True)
        o_ref[:, sl] = (_dot(p.astype(BF16), mv_ref[:, sl].astype(BF16)) / l).astype(o_ref.dtype)


def _xattn_cached_kernel(q_ref, mk_ref, mv_ref, o_ref):
    n_mem, heads, dh = mk_ref.shape
    tm = q_ref.shape[0]
    k_all = mk_ref[...].reshape(n_mem * heads, dh).astype(BF16)
    v_all = mv_ref[...].reshape(n_mem * heads, dh).astype(BF16)
    q_rows = jnp.concatenate([q_ref[:, h * dh:(h + 1) * dh] for h in range(heads)], axis=0)
    s = _dot_nt(q_rows.astype(BF16), k_all)
    row_head = lax.broadcasted_iota(jnp.int32, s.shape, 0) // tm
    col_head = lax.broadcasted_iota(jnp.int32, s.shape, 1) % heads
    s = jnp.where(row_head == col_head, s, MASKED)
    m = jnp.max(s, axis=1, keepdims=True)
    p = jnp.exp(s - m)
    l = jnp.sum(p, axis=1, keepdims=True)
    o_rows = _dot(p.astype(BF16), v_all) / l
    for h in range(heads):
        o_ref[:, h * dh:(h + 1) * dh] = o_rows[h * tm:(h + 1) * tm].astype(o_ref.dtype)


def _xattn_call(q2, mem_k, mem_v, k_index, v_index, *, rows_per_batch, tm):
    n, d = q2.shape
    nb = n // rows_per_batch
    tiles = rows_per_batch // tm
    qspec = pl.BlockSpec((tm, d), lambda b, t: (b * tiles + t, 0))
    if mem_k.ndim == 5:
        body = _xattn_cached_kernel
        mem_spec = lambda g: pl.BlockSpec((None, None) + mem_k.shape[2:], lambda b, t: (g, b, 0, 0, 0))
    else:
        body = functools.partial(_xattn_kernel, heads=X_HEADS)
        mem_spec = lambda g: pl.BlockSpec((None, mem_k.shape[1] // nb, d), lambda b, t: (g, b, 0))
    return pl.pallas_call(
        body, grid=(nb, tiles),
        in_specs=[qspec, mem_spec(k_index), mem_spec(v_index)],
        out_specs=qspec, out_shape=jax.ShapeDtypeStruct((n, d), q2.dtype),
        compiler_params=_params(("arbitrary", "arbitrary")), name="xattn",
    )(q2, mem_k, mem_v)


def _proj2_ffn_kernel(o_ref, x1_ref, wo_ref, g2_ref, b2_ref, wg_ref, wu_ref, wd_ref, g3_ref, b3_ref, out_ref,
                      *, alpha, ff_chunk):
    x2 = _layer_norm(alpha * x1_ref[...] + _dot(o_ref[...].astype(BF16), wo_ref[...]), g2_ref[...], b2_ref[...])
    xb = x2.astype(BF16)
    d_ff = wg_ref.shape[1]
    y = jnp.zeros(x2.shape, F32)
    for c in range(d_ff // ff_chunk):
        sl = slice(c * ff_chunk, (c + 1) * ff_chunk)
        gate = _dot(xb, wg_ref[:, sl])
        up = _dot(xb, wu_ref[:, sl])
        hid = gate * (1.0 / (1.0 + jnp.exp(-gate))) * up
        y = y + _dot(hid.astype(BF16), wd_ref[sl, :])
    out_ref[...] = _layer_norm(alpha * x2 + y, g3_ref[...], b3_ref[...])


def _proj2_ffn_call(o, x1, w_o, g2, b2, w_g, w_u, w_d, g3, b3, *, tm, alpha):
    n, d = x1.shape
    row = lambda i: (i, 0)
    consts = (w_o, g2, b2, w_g, w_u, w_d, g3, b3)
    return pl.pallas_call(
        functools.partial(_proj2_ffn_kernel, alpha=alpha, ff_chunk=256),
        grid=(n // tm,),
        in_specs=[pl.BlockSpec((tm, d), row), pl.BlockSpec((tm, d), row)] + [_resident(c.shape) for c in consts],
        out_specs=pl.BlockSpec((tm, d), row), out_shape=jax.ShapeDtypeStruct((n, d), F32),
        compiler_params=_params(("arbitrary",)), name="proj2_ffn",
    )(o, x1, *consts)


def _memkv_kernel(x_ref, w_ref, o_ref):
    o_ref[...] = _dot(x_ref[...].astype(BF16), w_ref[...])


def _memkv_call(mem, w_stack, *, tm):
    n, d = mem.shape
    g = w_stack.shape[0]
    return pl.pallas_call(
        _memkv_kernel, grid=(g, n // tm),
        in_specs=[pl.BlockSpec((tm, d), lambda j, i: (i, 0)), pl.BlockSpec((None, d, d), lambda j, i: (j, 0, 0))],
        out_specs=pl.BlockSpec((None, tm, d), lambda j, i: (j, i, 0)),
        out_shape=jax.ShapeDtypeStruct((g, n, d), F32),
        compiler_params=_params(("arbitrary", "arbitrary")), name="memkv",
    )(mem, w_stack)


def _block_diag(blocks):
    g, c, _ = blocks.shape
    eye = jnp.eye(g, dtype=blocks.dtype)
    return (eye[:, None, :, None] * blocks[:, :, None, :]).reshape(g * c, g * c)


def kernel(x_prompt, x_sample, mem_prompt, state_pool, cache_k, cache_v, cache_mem_k, cache_mem_v, page_table,
           w_in, pool_w, pool_scale, sg_norm_g, sg_norm_b, sg_w, sg_b, w_out, ln1_g, ln1_b, xq_w, xk_w, xv_w,
           xo_w, ln2_g, ln2_b, ffn_gate, ffn_up, ffn_down, ln3_g, ln3_b):
    n_layers = w_in.shape[0]
    bp, seq, d = x_prompt.shape
    bs, dec_seq, _ = x_sample.shape
    n_mem = mem_prompt.shape[1]
    n_phys = cache_k.shape[1]
    past_len = page_table.shape[1] * PAGE_SIZE
    assert past_len % MOBA_BLOCK == 0 and dec_seq <= MOBA_BLOCK and seq % ROW_TILE == 0
    alpha = (2 * n_layers) ** 0.25
    q_scale = (d // X_HEADS) ** -0.5

    xp = x_prompt.reshape(bp * seq, d)
    xs = x_sample.reshape(bs * dec_seq, d)
    paged = lambda c: jnp.transpose(c, (0, 1, 3, 4, 2)).reshape(n_layers * n_phys, MOBA_WIDTH, PAGE_SIZE)
    cache_k2, cache_v2 = paged(cache_k), paged(cache_v)
    hist16 = jnp.pad(state_pool, ((0, 0), (0, 0), (1, 0), (0, 0)))

    vec = lambda p: p.reshape(n_layers, 1, -1)
    w_in_b, w_out_b, xq_b, xo_b = (w.astype(BF16) for w in (w_in, w_out, xq_w, xo_w))
    wg_b, wu_b, wd_b = (w.astype(BF16) for w in (ffn_gate, ffn_up, ffn_down))
    pw_bd = jnp.stack([_block_diag(pool_w[l]) for l in range(n_layers)]).astype(BF16)
    causal = jnp.tril(jnp.ones((SG_CHUNK, SG_CHUNK), bool))
    sgw_p = jnp.where(causal, sg_w, 0.0).astype(BF16)
    sgb_p = jnp.repeat(jnp.swapaxes(sg_b, 1, 2), HEAD_DIM, axis=2)
    eye_s = jnp.eye(bs, dtype=F32)
    w_dec = jnp.where(causal[:dec_seq, :dec_seq], sg_w[:, :, :dec_seq, :dec_seq], 0.0)
    sgw_s = (eye_s[None, None, :, None, :, None] * w_dec[:, :, None, :, None, :]).reshape(
        n_layers, SG_HEADS, bs * dec_seq, bs * dec_seq).astype(BF16)
    sgb_s = jnp.tile(sgb_p[:, :dec_seq, :], (1, bs, 1))
    kv_stack = jnp.concatenate([xk_w, xv_w], axis=0).astype(BF16)

    memkv = _memkv_call(mem_prompt.reshape(bp * n_mem, d), kv_stack, tm=ROW_TILE)

    pool_p, pool_s, kp_l, vp_l, ks_l, vs_l, sgv_l = [], [], [], [], [], [], []
    for l in range(n_layers):
        ln = lambda p: vec(p)[l]
        yab, q, kt, vt, kb, hist = _in_prompt_call(
            xp, w_in_b[l], pw_bd[l], ln(pool_scale), ln(sg_norm_g), ln(sg_norm_b), sgw_p[l], sgb_p[l],
            seq=seq, tm=IN_ROW_TILE)
        ym = _moba_prompt_call(q, kb, vt, seq=seq, pairs_per_step=MOBA_PAIRS_PER_STEP)
        x1, q2 = _proj1_call(yab, ym, xp, w_out_b[l], ln(ln1_g), ln(ln1_b), xq_b[l], tm=IN_ROW_TILE, alpha=alpha,
                             q_scale=q_scale, q_dtype=BF16)
        o = _xattn_call(q2, memkv, memkv, l, n_layers + l, rows_per_batch=seq, tm=ROW_TILE)
        xp = _proj2_ffn_call(o, x1, xo_b[l], ln(ln2_g), ln(ln2_b), wg_b[l], wu_b[l], wd_b[l], ln(ln3_g),
                             ln(ln3_b), tm=ROW_TILE, alpha=alpha)
        pool_p.append(hist[:, 1:, :]); kp_l.append(kt); vp_l.append(vt)
        yab, q, k, v, hist, sgv = _in_sample_call(
            xs, hist16[l], w_in_b[l], pw_bd[l], ln(pool_scale), ln(sg_norm_g), ln(sg_norm_b), sgw_s[l], sgb_s[l],
            dec_seq=dec_seq, pos0=past_len)
        ym = _moba_decode_call(q, k, v, cache_k2, cache_v2, page_table, page_base=l * n_phys, dec_seq=dec_seq,
                               pps=DECODE_PAGES_PER_STEP)
        x1, q2 = _proj1_call(yab, ym, xs, w_out_b[l], ln(ln1_g), ln(ln1_b), xq_b[l], tm=bs * dec_seq, alpha=alpha,
                             q_scale=q_scale, q_dtype=F32)
        o = _xattn_call(q2, cache_mem_k, cache_mem_v, l, l, rows_per_batch=dec_seq, tm=dec_seq)
        xs = _proj2_ffn_call(o, x1, xo_b[l], ln(ln2_g), ln(ln2_b), wg_b[l], wu_b[l], wd_b[l], ln(ln3_g),
                             ln(ln3_b), tm=bs * dec_seq, alpha=alpha)
        pool_s.append(hist[:, 1:, :]); ks_l.append(k); vs_l.append(v); sgv_l.append(sgv)

    heads = lambda a, nb_, s_: a.reshape(nb_, s_, MOBA_HEADS, HEAD_DIM)
    tok_major = lambda a: jnp.transpose(a, (0, 1, 4, 2, 3))
    mem_shape = (n_layers, bp, n_mem, X_HEADS, d // X_HEADS)
    return (xp.reshape(bp, seq, d), xs.reshape(bs, dec_seq, d),
            jnp.stack(pool_p), jnp.stack(pool_s),
            tok_major(jnp.stack(kp_l)), tok_major(jnp.stack(vp_l)),
            jnp.stack([heads(a, bs, dec_seq) for a in ks_l]), jnp.stack([heads(a, bs, dec_seq) for a in vs_l]),
            jnp.stack([a.reshape(bs, dec_seq, SG_WIDTH) for a in sgv_l]),
            memkv[:n_layers].reshape(mem_shape), memkv[n_layers:].reshape(mem_shape))
```

```python
import functools

import jax
import jax.numpy as jnp
from jax import lax
from jax.experimental import pallas as pl
from jax.experimental.pallas import tpu as pltpu

F32 = jnp.float32
BF16 = jnp.bfloat16

HEAD_DIM = 64
POOL_WIDTH = 256
POOL_WINDOWS = (2, 4, 8, 16)
POOL_GROUP_DIM = POOL_WIDTH // len(POOL_WINDOWS)
POOL_HIST = max(POOL_WINDOWS) - 1
HIST_ROWS = POOL_HIST + 1
SG_WIDTH = 256
SG_HEADS = SG_WIDTH // HEAD_DIM
SG_CHUNK = 128
MOBA_WIDTH = 512
MOBA_HEADS = MOBA_WIDTH // HEAD_DIM
MOBA_BLOCK = 256
MOBA_TOPK = 3
PAGE_SIZE = 128
X_HEADS = 4
LN_EPS = 1e-5
MASKED = -1e30

LANES = 128
VMEM_LIMIT_BYTES = 56 * 1024 * 1024
ROW_TILE = 512
DECODE_PAGES_PER_STEP = 32
IN_ROW_TILE = 1024
MOBA_PAIRS_PER_STEP = 2


def _dot(a, b):
    return jnp.dot(a, b, preferred_element_type=F32)


def _dot_nt(a, b, precision=None):
    return lax.dot_general(a, b, (((1,), (1,)), ((), ())), preferred_element_type=F32, precision=precision)


def _layer_norm(y, g, b):
    mu = jnp.mean(y, axis=-1, keepdims=True)
    d = y - mu
    var = jnp.mean(d * d, axis=-1, keepdims=True)
    return d * lax.rsqrt(var + LN_EPS) * g + b


def _params(sem):
    return pltpu.CompilerParams(dimension_semantics=sem, vmem_limit_bytes=VMEM_LIMIT_BYTES)


def _resident(shape):
    nd = len(shape)
    return pl.BlockSpec(shape, lambda *_: (0,) * nd, pipeline_mode=pl.Buffered(1))


def _window_sums(ext):
    s2 = ext + pltpu.roll(ext, 1, 0)
    s4 = s2 + pltpu.roll(s2, 2, 0)
    s8 = s4 + pltpu.roll(s4, 4, 0)
    s16 = s8 + pltpu.roll(s8, 8, 0)
    lane = lax.broadcasted_iota(jnp.int32, ext.shape, 1)
    g = POOL_GROUP_DIM
    return jnp.where(lane < g, s2, jnp.where(lane < 2 * g, s4, jnp.where(lane < 3 * g, s8, s16)))


def _pool_out(sums, a, pos, pw_bd, scale):
    lane = lax.broadcasted_iota(jnp.int32, a.shape, 1)
    g = POOL_GROUP_DIM
    win = jnp.where(lane < g, 2, jnp.where(lane < 2 * g, 4, jnp.where(lane < 3 * g, 8, 16)))
    cnt = jnp.minimum(win, pos + 1).astype(F32)
    pooled = sums / cnt - a
    return _dot(pooled.astype(BF16), pw_bd) * scale


def _spatial_gate(zb, n_g, n_b, sgw_ref, bias, chunk):
    z = jax.nn.gelu(zb, approximate=True)
    u = z[:, :SG_WIDTH]
    v = _layer_norm(z[:, SG_WIDTH:], n_g, n_b)
    vb = v.astype(BF16)
    lane = lax.broadcasted_iota(jnp.int32, (chunk, LANES), 1)
    rows = []
    for c in range(zb.shape[0] // chunk):
        vc = vb[c * chunk:(c + 1) * chunk]
        cols = []
        for pair in range(SG_HEADS // 2):
            vp = vc[:, pair * LANES:(pair + 1) * LANES]
            s0 = _dot(sgw_ref[2 * pair], vp)
            s1 = _dot(sgw_ref[2 * pair + 1], vp)
            cols.append(jnp.where(lane < HEAD_DIM, s0, s1))
        rows.append(jnp.concatenate(cols, axis=1) + bias)
    s = rows[0] if len(rows) == 1 else jnp.concatenate(rows, axis=0)
    return u * s, v


def _project_qkv(xb, w_ref, q_ref, k_ref, v_ref, kb_ref, transposed):
    c0 = POOL_WIDTH + 2 * SG_WIDTH
    q = _dot(xb, w_ref[:, c0:c0 + MOBA_WIDTH]) * (HEAD_DIM ** -0.5)
    q_ref[...] = q.astype(q_ref.dtype)
    k = _dot(xb, w_ref[:, c0 + MOBA_WIDTH:c0 + 2 * MOBA_WIDTH])
    v = _dot(xb, w_ref[:, c0 + 2 * MOBA_WIDTH:c0 + 3 * MOBA_WIDTH])
    if transposed:
        k_ref[...] = k.T.reshape(k_ref.shape)
        v_ref[...] = v.T.reshape(v_ref.shape)
        kb_ref[...] = k.astype(BF16)
    else:
        k_ref[...] = k
        v_ref[...] = v


def _in_prompt_kernel(x_ref, w_ref, pw_ref, ps_ref, ng_ref, nb_ref, sgw_ref, sgb_ref, *rest, tiles_per_seq):
    yab_ref, q_ref, kt_ref, vt_ref, kb_ref, hist_ref, ext_ref = rest[-7:]
    tm = x_ref.shape[0]
    t = pl.program_id(0) % tiles_per_seq

    @pl.when(t == 0)
    def _():
        ext_ref[0:HIST_ROWS, :] = jnp.zeros((HIST_ROWS, POOL_WIDTH), F32)

    xb = x_ref[...].astype(BF16)
    a = _dot(xb, w_ref[:, 0:POOL_WIDTH])
    ext_ref[HIST_ROWS:, :] = a
    sums = _window_sums(ext_ref[...])[HIST_ROWS:]
    pos = t * tm + lax.broadcasted_iota(jnp.int32, a.shape, 0)
    yab_ref[:, 0:POOL_WIDTH] = _pool_out(sums, a, pos, pw_ref[...], ps_ref[...]).astype(yab_ref.dtype)
    tail = ext_ref[tm:tm + HIST_ROWS, :]
    hist_ref[...] = tail
    ext_ref[0:HIST_ROWS, :] = tail

    zb = _dot(xb, w_ref[:, POOL_WIDTH:POOL_WIDTH + 2 * SG_WIDTH])
    y_sg, _ = _spatial_gate(zb, ng_ref[...], nb_ref[...], sgw_ref, sgb_ref[...], SG_CHUNK)
    yab_ref[:, POOL_WIDTH:] = y_sg.astype(yab_ref.dtype)
    _project_qkv(xb, w_ref, q_ref, kt_ref, vt_ref, kb_ref, True)


def _in_prompt_call(x, w_in, pw_bd, pscale, n_g, n_b, sgw, sgb, kv_buffers, *, seq, tm, layer, n_layers):
    n, d = x.shape
    nb = n // seq
    tiles_per_seq = seq // tm
    row = lambda i: (i, 0)
    tok_minor = lambda i: (layer, i // tiles_per_seq, 0, 0, i % tiles_per_seq)
    kv_shape = jax.ShapeDtypeStruct((n_layers, nb, MOBA_HEADS, HEAD_DIM, seq), F32)
    out_shape = (
        jax.ShapeDtypeStruct((n, POOL_WIDTH + SG_WIDTH), BF16),
        jax.ShapeDtypeStruct((n, MOBA_WIDTH), BF16),
        kv_shape,
        kv_shape,
        jax.ShapeDtypeStruct((n, MOBA_WIDTH), BF16),
        jax.ShapeDtypeStruct((nb, HIST_ROWS, POOL_WIDTH), F32),
    )
    out_specs = (
        pl.BlockSpec((tm, POOL_WIDTH + SG_WIDTH), row),
        pl.BlockSpec((tm, MOBA_WIDTH), row),
        pl.BlockSpec((None, None, MOBA_HEADS, HEAD_DIM, tm), tok_minor),
        pl.BlockSpec((None, None, MOBA_HEADS, HEAD_DIM, tm), tok_minor),
        pl.BlockSpec((tm, MOBA_WIDTH), row),
        pl.BlockSpec((None, HIST_ROWS, POOL_WIDTH), lambda i: (i // tiles_per_seq, 0, 0)),
    )
    args = [x, w_in, pw_bd, pscale, n_g, n_b, sgw, sgb]
    in_specs = [pl.BlockSpec((tm, d), row)] + [_resident(a.shape) for a in args[1:]]
    aliases = {}
    if kv_buffers is not None:
        aliases = {len(args): 2, len(args) + 1: 3}
        args += list(kv_buffers)
        in_specs += [pl.BlockSpec(memory_space=pl.ANY)] * 2
    return pl.pallas_call(
        functools.partial(_in_prompt_kernel, tiles_per_seq=tiles_per_seq),
        grid=(n // tm,), in_specs=in_specs, out_specs=out_specs, out_shape=out_shape,
        input_output_aliases=aliases,
        scratch_shapes=[pltpu.VMEM((HIST_ROWS + tm, POOL_WIDTH), F32)],
        compiler_params=_params(("arbitrary",)), name="in_prompt",
    )(*args)


def _in_sample_kernel(x_ref, hist_in_ref, w_ref, pw_ref, ps_ref, ng_ref, nb_ref, sgw_ref, sgb_ref,
                      yab_ref, q_ref, k_ref, v_ref, hist_ref, sgv_ref, *, dec_seq, pos0):
    n = x_ref.shape[0]
    nb = n // dec_seq
    seg = HIST_ROWS + dec_seq
    xb = x_ref[...].astype(BF16)
    a = _dot(xb, w_ref[:, 0:POOL_WIDTH])
    ext = jnp.concatenate([hist_in_ref[...], a.reshape(nb, dec_seq, POOL_WIDTH)], axis=1)
    sums = _window_sums(ext.reshape(nb * seg, POOL_WIDTH)).reshape(nb, seg, POOL_WIDTH)
    sums = sums[:, HIST_ROWS:, :].reshape(n, POOL_WIDTH)
    pos = pos0 + lax.broadcasted_iota(jnp.int32, a.shape, 0) % dec_seq
    yab_ref[:, 0:POOL_WIDTH] = _pool_out(sums, a, pos, pw_ref[...], ps_ref[...]).astype(yab_ref.dtype)
    hist_ref[...] = ext[:, seg - HIST_ROWS:, :]

    zb = _dot(xb, w_ref[:, POOL_WIDTH:POOL_WIDTH + 2 * SG_WIDTH])
    y_sg, v_n = _spatial_gate(zb, ng_ref[...], nb_ref[...], sgw_ref, sgb_ref[...], n)
    yab_ref[:, POOL_WIDTH:] = y_sg.astype(yab_ref.dtype)
    sgv_ref[...] = v_n
    _project_qkv(xb, w_ref, q_ref, k_ref, v_ref, None, False)


def _in_sample_call(x, hist16, w_in, pw_bd, pscale, n_g, n_b, sgw_bd, sgb_t, *, dec_seq, pos0):
    n, d = x.shape
    nb = n // dec_seq
    out_shape = (
        jax.ShapeDtypeStruct((n, POOL_WIDTH + SG_WIDTH), BF16),
        jax.ShapeDtypeStruct((n, MOBA_WIDTH), F32),
        jax.ShapeDtypeStruct((n, MOBA_WIDTH), F32),
        jax.ShapeDtypeStruct((n, MOBA_WIDTH), F32),
        jax.ShapeDtypeStruct((nb, HIST_ROWS, POOL_WIDTH), F32),
        jax.ShapeDtypeStruct((n, SG_WIDTH), F32),
    )
    args = (x, hist16, w_in, pw_bd, pscale, n_g, n_b, sgw_bd, sgb_t)
    return pl.pallas_call(
        functools.partial(_in_sample_kernel, dec_seq=dec_seq, pos0=pos0),
        grid=(1,), in_specs=[_resident(a.shape) for a in args],
        out_specs=tuple(_resident(s.shape) for s in out_shape), out_shape=out_shape,
        compiler_params=_params(("arbitrary",)), name="in_sample",
    )(*args)


def _block_rank(g, blk, n_valid, n_blocks, axis):
    rank = jnp.zeros(g.shape, jnp.int32)
    for m in range(n_blocks):
        gm = lax.slice_in_dim(g, m, m + 1, axis=axis)
        beats = (gm > g) | ((gm == g) & (m < blk))
        rank = rank + jnp.where(beats & (m < n_valid), 1, 0)
    return rank


GATE_ROWS = 16


def _block_penalty(g, blk, n_valid):
    neg = -jnp.inf
    g = jnp.where(blk < n_valid, g, neg)
    pen = jnp.full(g.shape, MASKED, F32)
    for _ in range(MOBA_TOPK):
        mx = jnp.max(g, axis=0, keepdims=True)
        first = jnp.min(jnp.where(g == mx, blk, GATE_ROWS), axis=0, keepdims=True)
        pick = (blk == first) & (mx > neg)
        pen = jnp.where(pick, 0.0, pen)
        g = jnp.where(pick, neg, g)
    return pen


def _moba_prompt_kernel(q_ref, k_ref, vt_ref, o_ref, kaug_ref, vaug_ref, kmean_ref, qaug_ref, s_ref, smax_ref, m_ref,
                        acc_ref, *, n_blocks):
    blk_rows = MOBA_BLOCK
    qi = pl.program_id(2)
    n_pair = kaug_ref.shape[0]
    n_head = 2 * n_pair

    @pl.when(qi == 0)
    def _():
        lane = lax.broadcasted_iota(jnp.int32, (blk_rows, LANES), 1)
        sub = lax.broadcasted_iota(jnp.int32, (LANES, blk_rows), 0)
        lane1 = lax.broadcasted_iota(jnp.int32, (1, LANES), 1)
        kmean_ref[...] = jnp.zeros(kmean_ref.shape, F32)
        for hp in range(n_pair):
            for n in range(n_blocks):
                kb = k_ref[n * blk_rows:(n + 1) * blk_rows, hp * LANES:(hp + 1) * LANES]
                kaug_ref[hp, n * blk_rows:(n + 1) * blk_rows, 0:LANES] = kb
                kaug_ref[hp, n * blk_rows:(n + 1) * blk_rows, LANES:] = jnp.where(lane == n, 1.0, 0.0).astype(BF16)
                km = jnp.mean(kb.astype(F32), axis=0, keepdims=True)
                kmean_ref[hp, n:n + 1, :] = jnp.where(lane1 < HEAD_DIM, km, 0.0)
                kmean_ref[hp, GATE_ROWS + n:GATE_ROWS + n + 1, :] = jnp.where(lane1 < HEAD_DIM, 0.0, km)
                vt = vt_ref[2 * hp:2 * hp + 2, :, n * blk_rows:(n + 1) * blk_rows].reshape(LANES, blk_rows)
                vaug_ref[n, 2 * hp] = jnp.where(sub < HEAD_DIM, vt, 1.0).astype(BF16)
                vaug_ref[n, 2 * hp + 1] = jnp.where(sub >= HEAD_DIM, vt, 1.0).astype(BF16)
        seq = q_ref.shape[0]
        lane_q = lax.broadcasted_iota(jnp.int32, (seq, LANES), 1)
        blk = lax.broadcasted_iota(jnp.int32, (GATE_ROWS, seq), 0)
        own_blk = lax.broadcasted_iota(jnp.int32, (GATE_ROWS, seq), 1) // blk_rows
        for hp in range(n_pair):
            q = q_ref[:, hp * LANES:(hp + 1) * LANES]
            gate_t = _dot_nt(kmean_ref[hp], q.astype(F32), precision=lax.Precision.HIGHEST)
            for h in range(2):
                qh = jnp.where(lane_q // HEAD_DIM == h, q, jnp.zeros_like(q))
                pen_t = _block_penalty(gate_t[h * GATE_ROWS:(h + 1) * GATE_ROWS], blk, own_blk)
                pen_t = jnp.where(blk == own_blk, 0.0, pen_t)
                pen_t = jnp.concatenate([pen_t, jnp.zeros((LANES - GATE_ROWS, seq), F32)], axis=0)
                for n in range(n_blocks):
                    rows = slice(n * blk_rows, (n + 1) * blk_rows)
                    qaug_ref[n, 2 * hp + h] = jnp.concatenate(
                        [qh[rows], pen_t[:, rows].T.astype(BF16)], axis=1)

    def scores(t, slot):
        off = pl.multiple_of(t * 2 * blk_rows, 2 * blk_rows)
        for hp in range(n_pair):
            kj = kaug_ref[hp, pl.ds(off, 2 * blk_rows), :]
            for g in (2 * hp, 2 * hp + 1):
                st = _dot_nt(kj, qaug_ref[qi, g])
                s_ref[slot, g] = st
                smax_ref[slot, g] = jnp.max(st, axis=0, keepdims=True)

    def accumulate(t, slot):
        for g in range(n_head):
            st = s_ref[slot, g]
            m_prev = m_ref[g]
            m_new = jnp.maximum(m_prev, smax_ref[slot, g])
            p = jnp.exp(st - m_new).astype(BF16)
            pv = _dot(vaug_ref[2 * t, g], p[0:blk_rows]) + _dot(vaug_ref[2 * t + 1, g], p[blk_rows:])
            acc_ref[g] = jnp.exp(m_prev - m_new) * acc_ref[g] + pv
            m_ref[g] = m_new

    own = qi // 2
    scores(own, 0)
    scores(jnp.maximum(own - 1, 0), 1)
    key_i = lax.broadcasted_iota(jnp.int32, (2 * blk_rows, blk_rows), 0) - (qi % 2) * blk_rows
    qry_i = lax.broadcasted_iota(jnp.int32, (2 * blk_rows, blk_rows), 1)
    for g in range(n_head):
        st = jnp.where(key_i <= qry_i, s_ref[0, g], MASKED)
        m = jnp.max(st, axis=0, keepdims=True)
        p = jnp.exp(st - m).astype(BF16)
        acc_ref[g] = _dot(vaug_ref[2 * own, g], p[0:blk_rows]) + _dot(vaug_ref[2 * own + 1, g], p[blk_rows:])
        m_ref[g] = m

    def body(j, carry):
        scores(own - 2 - j, j % 2)
        accumulate(own - 1 - j, (1 + j) % 2)
        return carry

    lax.fori_loop(0, own - 1, body, 0)

    @pl.when(own > 0)
    def _():
        accumulate(0, own % 2)

    for hp in range(n_pair):
        a0 = acc_ref[2 * hp]
        a1 = acc_ref[2 * hp + 1]
        out_t = jnp.concatenate([a0[0:HEAD_DIM] / a0[HEAD_DIM:], a1[HEAD_DIM:] / a1[0:HEAD_DIM]], axis=0)
        o_ref[:, hp * LANES:(hp + 1) * LANES] = out_t.T.astype(o_ref.dtype)


def _moba_prompt_call(q, kb, vt, *, seq, pairs_per_step, layer):
    n = q.shape[0]
    nb = n // seq
    n_blocks = seq // MOBA_BLOCK
    assert n_blocks <= GATE_ROWS and n_blocks % 2 == 0
    steps = MOBA_WIDTH // (LANES * pairs_per_step)
    width = LANES * pairs_per_step
    heads = 2 * pairs_per_step
    ospec = pl.BlockSpec((MOBA_BLOCK, width), lambda b, hp, i: (b * n_blocks + i, hp))
    kspec = pl.BlockSpec((seq, width), lambda b, hp, i: (b, hp))
    vspec = pl.BlockSpec((None, None, heads, HEAD_DIM, seq), lambda b, hp, i: (layer, b, hp, 0, 0))
    return pl.pallas_call(
        functools.partial(_moba_prompt_kernel, n_blocks=n_blocks),
        grid=(nb, steps, n_blocks), in_specs=[kspec, kspec, vspec], out_specs=ospec,
        out_shape=jax.ShapeDtypeStruct((n, MOBA_WIDTH), BF16),
        scratch_shapes=[pltpu.VMEM((pairs_per_step, seq, 2 * LANES), BF16),
                        pltpu.VMEM((n_blocks, heads, LANES, MOBA_BLOCK), BF16),
                        pltpu.VMEM((pairs_per_step, 2 * GATE_ROWS, LANES), F32),
                        pltpu.VMEM((n_blocks, heads, MOBA_BLOCK, 2 * LANES), BF16),
                        pltpu.VMEM((2, heads, 2 * MOBA_BLOCK, MOBA_BLOCK), F32),
                        pltpu.VMEM((2, heads, 1, MOBA_BLOCK), F32),
                        pltpu.VMEM((heads, 1, MOBA_BLOCK), F32),
                        pltpu.VMEM((heads, LANES, MOBA_BLOCK), F32)],
        compiler_params=_params(("arbitrary", "arbitrary", "arbitrary")), name="moba_prompt",
    )(q, kb, vt)


def _moba_decode_kernel(pt_ref, q_ref, kn_ref, vn_ref, *rest, pps, n_steps, dec_seq):
    del pt_ref
    k_pages = rest[:pps]
    v_pages = rest[pps:2 * pps]
    o_ref = rest[2 * pps]
    qbd_ref, s_ref, p_ref, sown_ref, pown_ref, gate_ref, l_ref, r_ref = rest[2 * pps + 1:]
    ph = pl.program_id(1)
    st = pl.program_id(2)
    rows = MOBA_HEADS * dec_seq
    n_pages = pps * n_steps
    n_blocks = n_pages // 2
    lane = lax.broadcasted_iota(jnp.int32, (rows, LANES), 1)
    rowi = lax.broadcasted_iota(jnp.int32, (rows, LANES), 0)

    @pl.when((ph == 0) & (st == 0))
    def _():
        qt = jnp.concatenate([q_ref[...]] * MOBA_HEADS, axis=0)
        r2 = lax.broadcasted_iota(jnp.int32, qt.shape, 0)
        c2 = lax.broadcasted_iota(jnp.int32, qt.shape, 1)
        qbd = jnp.where(r2 // dec_seq == c2 // HEAD_DIM, qt, 0.0).astype(BF16)
        qbd_ref[...] = qbd
        gate_ref[...] = jnp.zeros(gate_ref.shape, F32)
        kn = jnp.concatenate([kn_ref[...], jnp.zeros((PAGE_SIZE - dec_seq, MOBA_WIDTH), F32)], axis=0)
        sown_ref[...] = _dot_nt(qbd, kn.astype(BF16))

    @pl.when(ph == 0)
    def _():
        qbd = qbd_ref[...]
        for i in range(0, pps, 2):
            s0 = _dot(qbd, k_pages[i][...].astype(BF16))
            s1 = _dot(qbd, k_pages[i + 1][...].astype(BF16))
            pg = st * pps + i
            s_ref[pg] = s0
            s_ref[pg + 1] = s1
            mean = jnp.sum(s0 + s1, axis=1, keepdims=True) * (1.0 / MOBA_BLOCK)
            gate_ref[...] = jnp.where(lane == pg // 2, mean, gate_ref[...])

    @pl.when((ph == 0) & (st == n_steps - 1))
    def _():
        g = gate_ref[...]
        rank = _block_rank(g, lane, n_blocks, n_blocks, 1)
        keep = (lane < n_blocks) & (rank < MOBA_TOPK)
        own_ok = (lane <= rowi % dec_seq) & (lane < dec_seq)
        s_own = jnp.where(own_ok, sown_ref[...], MASKED)
        m_run = s_own
        for n in range(n_blocks):
            kn = keep[:, n:n + 1]
            for pg in (2 * n, 2 * n + 1):
                m_run = jnp.maximum(m_run, jnp.where(kn, s_ref[pg], MASKED))
        m = jnp.max(m_run, axis=1, keepdims=True)
        p_own = jnp.exp(s_own - m)
        pown_ref[...] = p_own.astype(BF16)
        l_run = p_own
        for n in range(n_blocks):
            kn = keep[:, n:n + 1]
            for pg in (2 * n, 2 * n + 1):
                p = jnp.exp(jnp.where(kn, s_ref[pg], MASKED) - m)
                p_ref[pg] = p.astype(BF16)
                l_run = l_run + p
        l_ref[...] = jnp.sum(l_run, axis=1, keepdims=True)

    @pl.when((ph == 1) & (st == 0))
    def _():
        vn = jnp.concatenate([vn_ref[...], jnp.zeros((PAGE_SIZE - dec_seq, MOBA_WIDTH), F32)], axis=0)
        r_ref[...] = _dot(pown_ref[...], vn.astype(BF16))

    @pl.when(ph == 1)
    def _():
        acc = r_ref[...]
        for i in range(pps):
            acc = acc + _dot_nt(p_ref[st * pps + i], v_pages[i][...].astype(BF16))
        r_ref[...] = acc

    @pl.when((ph == 1) & (st == n_steps - 1))
    def _():
        r = r_ref[...] / l_ref[...]
        c2 = lax.broadcasted_iota(jnp.int32, (dec_seq, MOBA_WIDTH), 1)
        out = jnp.zeros((dec_seq, MOBA_WIDTH), F32)
        for h in range(MOBA_HEADS):
            out = jnp.where(c2 // HEAD_DIM == h, r[h * dec_seq:(h + 1) * dec_seq, :], out)
        o_ref[...] = out


def _moba_decode_call(q, k_new, v_new, cache_k, cache_v, page_table, *, page_base, dec_seq, pps):
    n = q.shape[0]
    nb = n // dec_seq
    n_pages = page_table.shape[1]
    assert n_pages % pps == 0 and pps % 2 == 0 and dec_seq % 8 == 0 and dec_seq <= PAGE_SIZE
    n_steps = n_pages // pps
    n_blocks = n_pages // 2
    assert n_blocks <= LANES
    rows = MOBA_HEADS * dec_seq
    base = page_base

    def k_map(i):
        def f(b, ph, st, pt):
            step = jnp.where(ph == 0, st, n_steps - 1)
            return (base + pt[b, step * pps + i], 0, 0)
        return f

    def v_map(i):
        def f(b, ph, st, pt):
            step = jnp.where(ph == 0, 0, st)
            return (base + pt[b, step * pps + i], 0, 0)
        return f

    tok = pl.BlockSpec((dec_seq, MOBA_WIDTH), lambda b, ph, st, pt: (b, 0))
    page = (None, MOBA_WIDTH, PAGE_SIZE)
    in_specs = ([tok, tok, tok] + [pl.BlockSpec(page, k_map(i)) for i in range(pps)]
                + [pl.BlockSpec(page, v_map(i)) for i in range(pps)])
    grid_spec = pltpu.PrefetchScalarGridSpec(
        num_scalar_prefetch=1, grid=(nb, 2, n_steps), in_specs=in_specs, out_specs=tok,
        scratch_shapes=[pltpu.VMEM((rows, MOBA_WIDTH), BF16),
                        pltpu.VMEM((n_pages, rows, LANES), F32),
                        pltpu.VMEM((n_pages, rows, LANES), BF16),
                        pltpu.VMEM((rows, LANES), F32), pltpu.VMEM((rows, LANES), BF16),
                        pltpu.VMEM((rows, LANES), F32), pltpu.VMEM((rows, 1), F32),
                        pltpu.VMEM((rows, MOBA_WIDTH), F32)])
    return pl.pallas_call(
        functools.partial(_moba_decode_kernel, pps=pps, n_steps=n_steps, dec_seq=dec_seq),
        grid_spec=grid_spec, out_shape=jax.ShapeDtypeStruct((n, MOBA_WIDTH), F32),
        compiler_params=_params(("arbitrary", "arbitrary", "arbitrary")), name="moba_decode",
    )(page_table, q, k_new, v_new, *([cache_k] * pps), *([cache_v] * pps))


def _proj1_kernel(yab_ref, ym_ref, x_ref, wo_ref, g_ref, b_ref, wq_ref, x1_ref, q2_ref, *, alpha, q_scale):
    half = yab_ref.shape[1]
    mix = _dot(yab_ref[...].astype(BF16), wo_ref[0:half, :]) + _dot(ym_ref[...].astype(BF16), wo_ref[half:, :])
    x1 = _layer_norm(alpha * x_ref[...] + mix, g_ref[...], b_ref[...])
    x1_ref[...] = x1
    q2_ref[...] = (_dot(x1.astype(BF16), wq_ref[...]) * q_scale).astype(q2_ref.dtype)


def _proj1_call(yab, ym, x, w_out, g, b, w_q, *, tm, alpha, q_scale, q_dtype):
    n, d = x.shape
    row = lambda i: (i, 0)
    return pl.pallas_call(
        functools.partial(_proj1_kernel, alpha=alpha, q_scale=q_scale),
        grid=(n // tm,),
        in_specs=[pl.BlockSpec((tm, yab.shape[1]), row), pl.BlockSpec((tm, ym.shape[1]), row),
                  pl.BlockSpec((tm, d), row), _resident(w_out.shape), _resident(g.shape), _resident(b.shape),
                  _resident(w_q.shape)],
        out_specs=(pl.BlockSpec((tm, d), row), pl.BlockSpec((tm, d), row)),
        out_shape=(jax.ShapeDtypeStruct((n, d), F32), jax.ShapeDtypeStruct((n, d), q_dtype)),
        compiler_params=_params(("arbitrary",)), name="proj1",
    )(yab, ym, x, w_out, g, b, w_q)


def _xattn_kernel(q_ref, mk_ref, mv_ref, o_ref, *, heads):
    dh = q_ref.shape[1] // heads
    for h in range(heads):
        sl = slice(h * dh, (h + 1) * dh)
        s = _dot_nt(q_ref[:, sl].astype(BF16), mk_ref[:, sl].astype(BF16))
        m = jnp.max(s, axis=1, keepdims=True)
        p = jnp.exp(s - m)
        l = jnp.sum(p, axis=1, keepdims=True)
        o_ref[:, sl] = (_dot(p.astype(BF16), mv_ref[:, sl].astype(BF16)) / l).astype(o_ref.dtype)


def _xattn_cached_kernel(q_ref, mk_ref, mv_ref, o_ref):
    n_mem, heads, dh = mk_ref.shape
    tm = q_ref.shape[0]
    k_all = mk_ref[...].reshape(n_mem * heads, dh).astype(BF16)
    v_all = mv_ref[...].reshape(n_mem * heads, dh).astype(BF16)
    q_rows = jnp.concatenate([q_ref[:, h * dh:(h + 1) * dh] for h in range(heads)], axis=0)
    s = _dot_nt(q_rows.astype(BF16), k_all)
    row_head = lax.broadcasted_iota(jnp.int32, s.shape, 0) // tm
    col_head = lax.broadcasted_iota(jnp.int32, s.shape, 1) % heads
    s = jnp.where(row_head == col_head, s, MASKED)
    m = jnp.max(s, axis=1, keepdims=True)
    p = jnp.exp(s - m)
    l = jnp.sum(p, axis=1, keepdims=True)
    o_rows = _dot(p.astype(BF16), v_all) / l
    for h in range(heads):
        o_ref[:, h * dh:(h + 1) * dh] = o_rows[h * tm:(h + 1) * tm].astype(o_ref.dtype)


def _xattn_call(q2, mem_k, mem_v, k_index, v_index, *, rows_per_batch, tm):
    n, d = q2.shape
    nb = n // rows_per_batch
    tiles = rows_per_batch // tm
    qspec = pl.BlockSpec((tm, d), lambda b, t: (b * tiles + t, 0))
    if mem_k.ndim == 5:
        body = _xattn_cached_kernel
        mem_spec = lambda g: pl.BlockSpec((None, None) + mem_k.shape[2:], lambda b, t: (g, b, 0, 0, 0))
    else:
        body = functools.partial(_xattn_kernel, heads=X_HEADS)
        mem_spec = lambda g: pl.BlockSpec((None, mem_k.shape[1] // nb, d), lambda b, t: (g, b, 0))
    return pl.pallas_call(
        body, grid=(nb, tiles),
        in_specs=[qspec, mem_spec(k_index), mem_spec(v_index)],
        out_specs=qspec, out_shape=jax.ShapeDtypeStruct((n, d), q2.dtype),
        compiler_params=_params(("arbitrary", "arbitrary")), name="xattn",
    )(q2, mem_k, mem_v)


def _proj2_ffn_kernel(o_ref, x1_ref, wo_ref, g2_ref, b2_ref, wg_ref, wu_ref, wd_ref, g3_ref, b3_ref, out_ref,
                      *, alpha, ff_chunk):
    x2 = _layer_norm(alpha * x1_ref[...] + _dot(o_ref[...].astype(BF16), wo_ref[...]), g2_ref[...], b2_ref[...])
    xb = x2.astype(BF16)
    d_ff = wg_ref.shape[1]
    y = jnp.zeros(x2.shape, F32)
    for c in range(d_ff // ff_chunk):
        sl = slice(c * ff_chunk, (c + 1) * ff_chunk)
        gate = _dot(xb, wg_ref[:, sl])
        up = _dot(xb, wu_ref[:, sl])
        hid = gate * (1.0 / (1.0 + jnp.exp(-gate))) * up
        y = y + _dot(hid.astype(BF16), wd_ref[sl, :])
    out_ref[...] = _layer_norm(alpha * x2 + y, g3_ref[...], b3_ref[...])


def _proj2_ffn_call(o, x1, w_o, g2, b2, w_g, w_u, w_d, g3, b3, *, tm, alpha):
    n, d = x1.shape
    row = lambda i: (i, 0)
    consts = (w_o, g2, b2, w_g, w_u, w_d, g3, b3)
    return pl.pallas_call(
        functools.partial(_proj2_ffn_kernel, alpha=alpha, ff_chunk=256),
        grid=(n // tm,),
        in_specs=[pl.BlockSpec((tm, d), row), pl.BlockSpec((tm, d), row)] + [_resident(c.shape) for c in consts],
        out_specs=pl.BlockSpec((tm, d), row), out_shape=jax.ShapeDtypeStruct((n, d), F32),
        compiler_params=_params(("arbitrary",)), name="proj2_ffn",
    )(o, x1, *consts)


def _memkv_kernel(x_ref, w_ref, o_ref):
    o_ref[...] = _dot(x_ref[...].astype(BF16), w_ref[...])


def _memkv_call(mem, w_stack, *, tm):
    n, d = mem.shape
    g = w_stack.shape[0]
    return pl.pallas_call(
        _memkv_kernel, grid=(g, n // tm),
        in_specs=[pl.BlockSpec((tm, d), lambda j, i: (i, 0)), pl.BlockSpec((None, d, d), lambda j, i: (j, 0, 0))],
        out_specs=pl.BlockSpec((None, tm, d), lambda j, i: (j, i, 0)),
        out_shape=jax.ShapeDtypeStruct((g, n, d), F32),
        compiler_params=_params(("arbitrary", "arbitrary")), name="memkv",
    )(mem, w_stack)


def _block_diag(blocks):
    g, c, _ = blocks.shape
    eye = jnp.eye(g, dtype=blocks.dtype)
    return (eye[:, None, :, None] * blocks[:, :, None, :]).reshape(g * c, g * c)


def kernel(x_prompt, x_sample, mem_prompt, state_pool, cache_k, cache_v, cache_mem_k, cache_mem_v, page_table,
           w_in, pool_w, pool_scale, sg_norm_g, sg_norm_b, sg_w, sg_b, w_out, ln1_g, ln1_b, xq_w, xk_w, xv_w,
           xo_w, ln2_g, ln2_b, ffn_gate, ffn_up, ffn_down, ln3_g, ln3_b):
    n_layers = w_in.shape[0]
    bp, seq, d = x_prompt.shape
    bs, dec_seq, _ = x_sample.shape
    n_mem = mem_prompt.shape[1]
    n_phys = cache_k.shape[1]
    past_len = page_table.shape[1] * PAGE_SIZE
    assert past_len % MOBA_BLOCK == 0 and dec_seq <= MOBA_BLOCK and seq % ROW_TILE == 0
    alpha = (2 * n_layers) ** 0.25
    q_scale = (d // X_HEADS) ** -0.5

    xp = x_prompt.reshape(bp * seq, d)
    xs = x_sample.reshape(bs * dec_seq, d)
    paged = lambda c: jnp.transpose(c, (0, 1, 3, 4, 2)).reshape(n_layers * n_phys, MOBA_WIDTH, PAGE_SIZE)
    cache_k2, cache_v2 = paged(cache_k), paged(cache_v)
    hist16 = jnp.pad(state_pool, ((0, 0), (0, 0), (1, 0), (0, 0)))

    vec = lambda p: p.reshape(n_layers, 1, -1)
    w_in_b, w_out_b, xq_b, xo_b = (w.astype(BF16) for w in (w_in, w_out, xq_w, xo_w))
    wg_b, wu_b, wd_b = (w.astype(BF16) for w in (ffn_gate, ffn_up, ffn_down))
    pw_bd = jnp.stack([_block_diag(pool_w[l]) for l in range(n_layers)]).astype(BF16)
    causal = jnp.tril(jnp.ones((SG_CHUNK, SG_CHUNK), bool))
    sgw_p = jnp.where(causal, sg_w, 0.0).astype(BF16)
    sgb_p = jnp.repeat(jnp.swapaxes(sg_b, 1, 2), HEAD_DIM, axis=2)
    eye_s = jnp.eye(bs, dtype=F32)
    w_dec = jnp.where(causal[:dec_seq, :dec_seq], sg_w[:, :, :dec_seq, :dec_seq], 0.0)
    sgw_s = (eye_s[None, None, :, None, :, None] * w_dec[:, :, None, :, None, :]).reshape(
        n_layers, SG_HEADS, bs * dec_seq, bs * dec_seq).astype(BF16)
    sgb_s = jnp.tile(sgb_p[:, :dec_seq, :], (1, bs, 1))
    kv_stack = jnp.concatenate([xk_w, xv_w], axis=0).astype(BF16)

    memkv = _memkv_call(mem_prompt.reshape(bp * n_mem, d), kv_stack, tm=ROW_TILE)

    pool_p, pool_s, ks_l, vs_l, sgv_l = [], [], [], [], []
    kv_new = None
    for l in range(n_layers):
        ln = lambda p: vec(p)[l]
        yab, q, kt, vt, kb, hist = _in_prompt_call(
            xp, w_in_b[l], pw_bd[l], ln(pool_scale), ln(sg_norm_g), ln(sg_norm_b), sgw_p[l], sgb_p[l],
            kv_new, seq=seq, tm=IN_ROW_TILE, layer=l, n_layers=n_layers)
        kv_new = (kt, vt)
        ym = _moba_prompt_call(q, kb, vt, seq=seq, pairs_per_step=MOBA_PAIRS_PER_STEP, layer=l)
        x1, q2 = _proj1_call(yab, ym, xp, w_out_b[l], ln(ln1_g), ln(ln1_b), xq_b[l], tm=IN_ROW_TILE, alpha=alpha,
                             q_scale=q_scale, q_dtype=BF16)
        o = _xattn_call(q2, memkv, memkv, l, n_layers + l, rows_per_batch=seq, tm=ROW_TILE)
        xp = _proj2_ffn_call(o, x1, xo_b[l], ln(ln2_g), ln(ln2_b), wg_b[l], wu_b[l], wd_b[l], ln(ln3_g),
                             ln(ln3_b), tm=ROW_TILE, alpha=alpha)
        pool_p.append(hist[:, 1:, :])
        yab, q, k, v, hist, sgv = _in_sample_call(
            xs, hist16[l], w_in_b[l], pw_bd[l], ln(pool_scale), ln(sg_norm_g), ln(sg_norm_b), sgw_s[l], sgb_s[l],
            dec_seq=dec_seq, pos0=past_len)
        ym = _moba_decode_call(q, k, v, cache_k2, cache_v2, page_table, page_base=l * n_phys, dec_seq=dec_seq,
                               pps=DECODE_PAGES_PER_STEP)
        x1, q2 = _proj1_call(yab, ym, xs, w_out_b[l], ln(ln1_g), ln(ln1_b), xq_b[l], tm=bs * dec_seq, alpha=alpha,
                             q_scale=q_scale, q_dtype=F32)
        o = _xattn_call(q2, cache_mem_k, cache_mem_v, l, l, rows_per_batch=dec_seq, tm=dec_seq)
        xs = _proj2_ffn_call(o, x1, xo_b[l], ln(ln2_g), ln(ln2_b), wg_b[l], wu_b[l], wd_b[l], ln(ln3_g),
                             ln(ln3_b), tm=bs * dec_seq, alpha=alpha)
        pool_s.append(hist[:, 1:, :]); ks_l.append(k); vs_l.append(v); sgv_l.append(sgv)

    heads = lambda a, nb_, s_: a.reshape(nb_, s_, MOBA_HEADS, HEAD_DIM)
    tok_major = lambda a: jnp.transpose(a, (0, 1, 4, 2, 3))
    mem_shape = (n_layers, bp, n_mem, X_HEADS, d // X_HEADS)
    return (xp.reshape(bp, seq, d), xs.reshape(bs, dec_seq, d),
            jnp.stack(pool_p), jnp.stack(pool_s),
            tok_major(kv_new[0]), tok_major(kv_new[1]),
            jnp.stack([heads(a, bs, dec_seq) for a in ks_l]), jnp.stack([heads(a, bs, dec_seq) for a in vs_l]),
            jnp.stack([a.reshape(bs, dec_seq, SG_WIDTH) for a in sgv_l]),
            memkv[:n_layers].reshape(mem_shape), memkv[n_layers:].reshape(mem_shape))
```

```python
import functools

import jax
import jax.numpy as jnp
from jax import lax
from jax.experimental import pallas as pl
from jax.experimental.pallas import tpu as pltpu

F32 = jnp.float32
BF16 = jnp.bfloat16

HEAD_DIM = 64
POOL_WIDTH = 256
POOL_WINDOWS = (2, 4, 8, 16)
POOL_GROUP_DIM = POOL_WIDTH // len(POOL_WINDOWS)
POOL_HIST = max(POOL_WINDOWS) - 1
HIST_ROWS = POOL_HIST + 1
SG_WIDTH = 256
SG_HEADS = SG_WIDTH // HEAD_DIM
SG_CHUNK = 128
MOBA_WIDTH = 512
MOBA_HEADS = MOBA_WIDTH // HEAD_DIM
MOBA_BLOCK = 256
MOBA_TOPK = 3
PAGE_SIZE = 128
X_HEADS = 4
LN_EPS = 1e-5
MASKED = -1e30

LANES = 128
VMEM_LIMIT_BYTES = 56 * 1024 * 1024
ROW_TILE = 512
DECODE_PAGES_PER_STEP = 32
IN_ROW_TILE = 1024
MOBA_PAIRS_PER_STEP = 2


def _dot(a, b):
    return jnp.dot(a, b, preferred_element_type=F32)


def _dot_nt(a, b, precision=None):
    return lax.dot_general(a, b, (((1,), (1,)), ((), ())), preferred_element_type=F32, precision=precision)


def _layer_norm(y, g, b):
    mu = jnp.mean(y, axis=-1, keepdims=True)
    d = y - mu
    var = jnp.mean(d * d, axis=-1, keepdims=True)
    return d * lax.rsqrt(var + LN_EPS) * g + b


def _params(sem):
    return pltpu.CompilerParams(dimension_semantics=sem, vmem_limit_bytes=VMEM_LIMIT_BYTES)


def _resident(shape):
    nd = len(shape)
    return pl.BlockSpec(shape, lambda *_: (0,) * nd, pipeline_mode=pl.Buffered(1))


def _window_sums(ext):
    s2 = ext + pltpu.roll(ext, 1, 0)
    s4 = s2 + pltpu.roll(s2, 2, 0)
    s8 = s4 + pltpu.roll(s4, 4, 0)
    s16 = s8 + pltpu.roll(s8, 8, 0)
    lane = lax.broadcasted_iota(jnp.int32, ext.shape, 1)
    g = POOL_GROUP_DIM
    return jnp.where(lane < g, s2, jnp.where(lane < 2 * g, s4, jnp.where(lane < 3 * g, s8, s16)))


def _pool_out(sums, a, pos, pw_bd, scale):
    lane = lax.broadcasted_iota(jnp.int32, a.shape, 1)
    g = POOL_GROUP_DIM
    win = jnp.where(lane < g, 2, jnp.where(lane < 2 * g, 4, jnp.where(lane < 3 * g, 8, 16)))
    cnt = jnp.minimum(win, pos + 1).astype(F32)
    pooled = sums / cnt - a
    return _dot(pooled.astype(BF16), pw_bd) * scale


def _spatial_gate(zb, n_g, n_b, sgw_ref, bias, chunk):
    z = jax.nn.gelu(zb, approximate=True)
    u = z[:, :SG_WIDTH]
    v = _layer_norm(z[:, SG_WIDTH:], n_g, n_b)
    vb = v.astype(BF16)
    lane = lax.broadcasted_iota(jnp.int32, (chunk, LANES), 1)
    rows = []
    for c in range(zb.shape[0] // chunk):
        vc = vb[c * chunk:(c + 1) * chunk]
        cols = []
        for pair in range(SG_HEADS // 2):
            vp = vc[:, pair * LANES:(pair + 1) * LANES]
            s0 = _dot(sgw_ref[2 * pair], vp)
            s1 = _dot(sgw_ref[2 * pair + 1], vp)
            cols.append(jnp.where(lane < HEAD_DIM, s0, s1))
        rows.append(jnp.concatenate(cols, axis=1) + bias)
    s = rows[0] if len(rows) == 1 else jnp.concatenate(rows, axis=0)
    return u * s, v


def _project_qkv(xb, w_ref, q_ref, k_ref, v_ref, kb_ref, transposed):
    c0 = POOL_WIDTH + 2 * SG_WIDTH
    q = _dot(xb, w_ref[:, c0:c0 + MOBA_WIDTH]) * (HEAD_DIM ** -0.5)
    q_ref[...] = q.astype(q_ref.dtype)
    k = _dot(xb, w_ref[:, c0 + MOBA_WIDTH:c0 + 2 * MOBA_WIDTH])
    v = _dot(xb, w_ref[:, c0 + 2 * MOBA_WIDTH:c0 + 3 * MOBA_WIDTH])
    if transposed:
        k_ref[...] = k.T.reshape(k_ref.shape)
        v_ref[...] = v.T.reshape(v_ref.shape)
        kb_ref[...] = k.astype(BF16)
    else:
        k_ref[...] = k
        v_ref[...] = v


def _in_prompt_kernel(x_ref, w_ref, pw_ref, ps_ref, ng_ref, nb_ref, sgw_ref, sgb_ref, *rest, tiles_per_seq):
    yab_ref, q_ref, kt_ref, vt_ref, kb_ref, hist_ref, ext_ref = rest[-7:]
    tm = x_ref.shape[0]
    t = pl.program_id(0) % tiles_per_seq

    @pl.when(t == 0)
    def _():
        ext_ref[0:HIST_ROWS, :] = jnp.zeros((HIST_ROWS, POOL_WIDTH), F32)

    xb = x_ref[...].astype(BF16)
    a = _dot(xb, w_ref[:, 0:POOL_WIDTH])
    ext_ref[HIST_ROWS:, :] = a
    sums = _window_sums(ext_ref[...])[HIST_ROWS:]
    pos = t * tm + lax.broadcasted_iota(jnp.int32, a.shape, 0)
    yab_ref[:, 0:POOL_WIDTH] = _pool_out(sums, a, pos, pw_ref[...], ps_ref[...]).astype(yab_ref.dtype)
    tail = ext_ref[tm:tm + HIST_ROWS, :]
    hist_ref[...] = tail
    ext_ref[0:HIST_ROWS, :] = tail

    zb = _dot(xb, w_ref[:, POOL_WIDTH:POOL_WIDTH + 2 * SG_WIDTH])
    y_sg, _ = _spatial_gate(zb, ng_ref[...], nb_ref[...], sgw_ref, sgb_ref[...], SG_CHUNK)
    yab_ref[:, POOL_WIDTH:] = y_sg.astype(yab_ref.dtype)
    _project_qkv(xb, w_ref, q_ref, kt_ref, vt_ref, kb_ref, True)


def _in_prompt_call(x, w_in, pw_bd, pscale, n_g, n_b, sgw, sgb, kv_buffers, *, seq, tm, layer, n_layers):
    n, d = x.shape
    nb = n // seq
    tiles_per_seq = seq // tm
    row = lambda i: (i, 0)
    tok_minor = lambda i: (layer, i // tiles_per_seq, 0, 0, i % tiles_per_seq)
    kv_shape = jax.ShapeDtypeStruct((n_layers, nb, MOBA_HEADS, HEAD_DIM, seq), F32)
    out_shape = (
        jax.ShapeDtypeStruct((n, POOL_WIDTH + SG_WIDTH), BF16),
        jax.ShapeDtypeStruct((n, MOBA_WIDTH), BF16),
        kv_shape,
        kv_shape,
        jax.ShapeDtypeStruct((n, MOBA_WIDTH), BF16),
        jax.ShapeDtypeStruct((nb, HIST_ROWS, POOL_WIDTH), F32),
    )
    out_specs = (
        pl.BlockSpec((tm, POOL_WIDTH + SG_WIDTH), row),
        pl.BlockSpec((tm, MOBA_WIDTH), row),
        pl.BlockSpec((None, None, MOBA_HEADS, HEAD_DIM, tm), tok_minor),
        pl.BlockSpec((None, None, MOBA_HEADS, HEAD_DIM, tm), tok_minor),
        pl.BlockSpec((tm, MOBA_WIDTH), row),
        pl.BlockSpec((None, HIST_ROWS, POOL_WIDTH), lambda i: (i // tiles_per_seq, 0, 0)),
    )
    args = [x, w_in, pw_bd, pscale, n_g, n_b, sgw, sgb]
    in_specs = [pl.BlockSpec((tm, d), row)] + [_resident(a.shape) for a in args[1:]]
    aliases = {}
    if kv_buffers is not None:
        aliases = {len(args): 2, len(args) + 1: 3}
        args += list(kv_buffers)
        in_specs += [pl.BlockSpec(memory_space=pl.ANY)] * 2
    return pl.pallas_call(
        functools.partial(_in_prompt_kernel, tiles_per_seq=tiles_per_seq),
        grid=(n // tm,), in_specs=in_specs, out_specs=out_specs, out_shape=out_shape,
        input_output_aliases=aliases,
        scratch_shapes=[pltpu.VMEM((HIST_ROWS + tm, POOL_WIDTH), F32)],
        compiler_params=_params(("arbitrary",)), name="in_prompt",
    )(*args)


def _in_sample_kernel(x_ref, hist_in_ref, w_ref, pw_ref, ps_ref, ng_ref, nb_ref, sgw_ref, sgb_ref,
                      yab_ref, q_ref, k_ref, v_ref, hist_ref, sgv_ref, *, dec_seq, pos0):
    n = x_ref.shape[0]
    nb = n // dec_seq
    seg = HIST_ROWS + dec_seq
    xb = x_ref[...].astype(BF16)
    a = _dot(xb, w_ref[:, 0:POOL_WIDTH])
    ext = jnp.concatenate([hist_in_ref[...], a.reshape(nb, dec_seq, POOL_WIDTH)], axis=1)
    sums = _window_sums(ext.reshape(nb * seg, POOL_WIDTH)).reshape(nb, seg, POOL_WIDTH)
    sums = sums[:, HIST_ROWS:, :].reshape(n, POOL_WIDTH)
    pos = pos0 + lax.broadcasted_iota(jnp.int32, a.shape, 0) % dec_seq
    yab_ref[:, 0:POOL_WIDTH] = _pool_out(sums, a, pos, pw_ref[...], ps_ref[...]).astype(yab_ref.dtype)
    hist_ref[...] = ext[:, seg - HIST_ROWS:, :]

    zb = _dot(xb, w_ref[:, POOL_WIDTH:POOL_WIDTH + 2 * SG_WIDTH])
    y_sg, v_n = _spatial_gate(zb, ng_ref[...], nb_ref[...], sgw_ref, sgb_ref[...], n)
    yab_ref[:, POOL_WIDTH:] = y_sg.astype(yab_ref.dtype)
    sgv_ref[...] = v_n
    _project_qkv(xb, w_ref, q_ref, k_ref, v_ref, None, False)


def _in_sample_call(x, hist16, w_in, pw_bd, pscale, n_g, n_b, sgw_bd, sgb_t, *, dec_seq, pos0):
    n, d = x.shape
    nb = n // dec_seq
    out_shape = (
        jax.ShapeDtypeStruct((n, POOL_WIDTH + SG_WIDTH), BF16),
        jax.ShapeDtypeStruct((n, MOBA_WIDTH), F32),
        jax.ShapeDtypeStruct((n, MOBA_WIDTH), F32),
        jax.ShapeDtypeStruct((n, MOBA_WIDTH), F32),
        jax.ShapeDtypeStruct((nb, HIST_ROWS, POOL_WIDTH), F32),
        jax.ShapeDtypeStruct((n, SG_WIDTH), F32),
    )
    args = (x, hist16, w_in, pw_bd, pscale, n_g, n_b, sgw_bd, sgb_t)
    return pl.pallas_call(
        functools.partial(_in_sample_kernel, dec_seq=dec_seq, pos0=pos0),
        grid=(1,), in_specs=[_resident(a.shape) for a in args],
        out_specs=tuple(_resident(s.shape) for s in out_shape), out_shape=out_shape,
        compiler_params=_params(("arbitrary",)), name="in_sample",
    )(*args)


def _block_rank(g, blk, n_valid, n_blocks, axis):
    rank = jnp.zeros(g.shape, jnp.int32)
    for m in range(n_blocks):
        gm = lax.slice_in_dim(g, m, m + 1, axis=axis)
        beats = (gm > g) | ((gm == g) & (m < blk))
        rank = rank + jnp.where(beats & (m < n_valid), 1, 0)
    return rank


GATE_ROWS = 16


def _block_penalty(g, blk, n_valid):
    neg = -jnp.inf
    g = jnp.where(blk < n_valid, g, neg)
    pen = jnp.full(g.shape, MASKED, F32)
    for _ in range(MOBA_TOPK):
        mx = jnp.max(g, axis=0, keepdims=True)
        first = jnp.min(jnp.where(g == mx, blk, GATE_ROWS), axis=0, keepdims=True)
        pick = (blk == first) & (mx > neg)
        pen = jnp.where(pick, 0.0, pen)
        g = jnp.where(pick, neg, g)
    return pen


def _moba_prompt_kernel(q_ref, k_ref, vt_ref, o_ref, kaug_ref, vaug_ref, kmean_ref, qaug_ref, s_ref, smax_ref, m_ref,
                        acc_ref, *, n_blocks):
    blk_rows = MOBA_BLOCK
    qi = pl.program_id(2)
    n_pair = kaug_ref.shape[0]
    n_head = 2 * n_pair

    @pl.when(qi == 0)
    def _():
        lane = lax.broadcasted_iota(jnp.int32, (blk_rows, LANES), 1)
        sub = lax.broadcasted_iota(jnp.int32, (LANES, blk_rows), 0)
        lane1 = lax.broadcasted_iota(jnp.int32, (1, LANES), 1)
        kmean_ref[...] = jnp.zeros(kmean_ref.shape, F32)
        for hp in range(n_pair):
            for n in range(n_blocks):
                kb = k_ref[n * blk_rows:(n + 1) * blk_rows, hp * LANES:(hp + 1) * LANES]
                kaug_ref[hp, n * blk_rows:(n + 1) * blk_rows, 0:LANES] = kb
                kaug_ref[hp, n * blk_rows:(n + 1) * blk_rows, LANES:] = jnp.where(lane == n, 1.0, 0.0).astype(BF16)
                km = jnp.mean(kb.astype(F32), axis=0, keepdims=True)
                kmean_ref[hp, n:n + 1, :] = jnp.where(lane1 < HEAD_DIM, km, 0.0)
                kmean_ref[hp, GATE_ROWS + n:GATE_ROWS + n + 1, :] = jnp.where(lane1 < HEAD_DIM, 0.0, km)
                vt = vt_ref[2 * hp:2 * hp + 2, :, n * blk_rows:(n + 1) * blk_rows].reshape(LANES, blk_rows)
                vaug_ref[n, 2 * hp] = jnp.where(sub < HEAD_DIM, vt, 1.0).astype(BF16)
                vaug_ref[n, 2 * hp + 1] = jnp.where(sub >= HEAD_DIM, vt, 1.0).astype(BF16)
        seq = q_ref.shape[0]
        lane_q = lax.broadcasted_iota(jnp.int32, (seq, LANES), 1)
        blk = lax.broadcasted_iota(jnp.int32, (GATE_ROWS, seq), 0)
        own_blk = lax.broadcasted_iota(jnp.int32, (GATE_ROWS, seq), 1) // blk_rows
        for hp in range(n_pair):
            q = q_ref[:, hp * LANES:(hp + 1) * LANES]
            gate_t = _dot_nt(kmean_ref[hp], q.astype(F32), precision=lax.Precision.HIGHEST)
            for h in range(2):
                qh = jnp.where(lane_q // HEAD_DIM == h, q, jnp.zeros_like(q))
                pen_t = _block_penalty(gate_t[h * GATE_ROWS:(h + 1) * GATE_ROWS], blk, own_blk)
                pen_t = jnp.where(blk == own_blk, 0.0, pen_t)
                pen_t = jnp.concatenate([pen_t, jnp.zeros((LANES - GATE_ROWS, seq), F32)], axis=0)
                for n in range(n_blocks):
                    rows = slice(n * blk_rows, (n + 1) * blk_rows)
                    qaug_ref[n, 2 * hp + h] = jnp.concatenate(
                        [qh[rows], pen_t[:, rows].T.astype(BF16)], axis=1)

    def scores(t, slot):
        off = pl.multiple_of(t * 2 * blk_rows, 2 * blk_rows)
        for hp in range(n_pair):
            kj = kaug_ref[hp, pl.ds(off, 2 * blk_rows), :]
            for g in (2 * hp, 2 * hp + 1):
                st = _dot_nt(kj, qaug_ref[qi, g])
                s_ref[slot, g] = st
                smax_ref[slot, g] = jnp.max(st, axis=0, keepdims=True)

    def accumulate(t, slot):
        for g in range(n_head):
            st = s_ref[slot, g]
            m_prev = m_ref[g]
            m_new = jnp.maximum(m_prev, smax_ref[slot, g])
            p = jnp.exp(st - m_new).astype(BF16)
            pv = _dot(vaug_ref[2 * t, g], p[0:blk_rows]) + _dot(vaug_ref[2 * t + 1, g], p[blk_rows:])
            acc_ref[g] = jnp.exp(m_prev - m_new) * acc_ref[g] + pv
            m_ref[g] = m_new

    own = qi // 2
    scores(own, 0)
    scores(jnp.maximum(own - 1, 0), 1)
    key_i = lax.broadcasted_iota(jnp.int32, (2 * blk_rows, blk_rows), 0) - (qi % 2) * blk_rows
    qry_i = lax.broadcasted_iota(jnp.int32, (2 * blk_rows, blk_rows), 1)
    for g in range(n_head):
        st = jnp.where(key_i <= qry_i, s_ref[0, g], MASKED)
        m = jnp.max(st, axis=0, keepdims=True)
        p = jnp.exp(st - m).astype(BF16)
        acc_ref[g] = _dot(vaug_ref[2 * own, g], p[0:blk_rows]) + _dot(vaug_ref[2 * own + 1, g], p[blk_rows:])
        m_ref[g] = m

    def body(j, carry):
        scores(own - 2 - j, j % 2)
        accumulate(own - 1 - j, (1 + j) % 2)
        return carry

    lax.fori_loop(0, own - 1, body, 0)

    @pl.when(own > 0)
    def _():
        accumulate(0, own % 2)

    for hp in range(n_pair):
        a0 = acc_ref[2 * hp]
        a1 = acc_ref[2 * hp + 1]
        out_t = jnp.concatenate([a0[0:HEAD_DIM] / a0[HEAD_DIM:], a1[HEAD_DIM:] / a1[0:HEAD_DIM]], axis=0)
        o_ref[:, hp * LANES:(hp + 1) * LANES] = out_t.T.astype(o_ref.dtype)


def _moba_prompt_call(q, kb, vt, *, seq, pairs_per_step, layer):
    n = q.shape[0]
    nb = n // seq
    n_blocks = seq // MOBA_BLOCK
    assert n_blocks <= GATE_ROWS and n_blocks % 2 == 0
    steps = MOBA_WIDTH // (LANES * pairs_per_step)
    width = LANES * pairs_per_step
    heads = 2 * pairs_per_step
    ospec = pl.BlockSpec((MOBA_BLOCK, width), lambda b, hp, i: (b * n_blocks + i, hp))
    kspec = pl.BlockSpec((seq, width), lambda b, hp, i: (b, hp))
    vspec = pl.BlockSpec((None, None, heads, HEAD_DIM, seq), lambda b, hp, i: (layer, b, hp, 0, 0))
    return pl.pallas_call(
        functools.partial(_moba_prompt_kernel, n_blocks=n_blocks),
        grid=(nb, steps, n_blocks), in_specs=[kspec, kspec, vspec], out_specs=ospec,
        out_shape=jax.ShapeDtypeStruct((n, MOBA_WIDTH), BF16),
        scratch_shapes=[pltpu.VMEM((pairs_per_step, seq, 2 * LANES), BF16),
                        pltpu.VMEM((n_blocks, heads, LANES, MOBA_BLOCK), BF16),
                        pltpu.VMEM((pairs_per_step, 2 * GATE_ROWS, LANES), F32),
                        pltpu.VMEM((n_blocks, heads, MOBA_BLOCK, 2 * LANES), BF16),
                        pltpu.VMEM((2, heads, 2 * MOBA_BLOCK, MOBA_BLOCK), F32),
                        pltpu.VMEM((2, heads, 1, MOBA_BLOCK), F32),
                        pltpu.VMEM((heads, 1, MOBA_BLOCK), F32),
                        pltpu.VMEM((heads, LANES, MOBA_BLOCK), F32)],
        compiler_params=_params(("arbitrary", "arbitrary", "arbitrary")), name="moba_prompt",
    )(q, kb, vt)


def _moba_decode_probs_kernel(pt_ref, q_ref, kn_ref, *rest, pps, n_steps, dec_seq):
    del pt_ref
    k_pages = rest[:pps]
    p_ref, pown_ref, l_ref = rest[pps:pps + 3]
    qbd_ref, s_ref, sown_ref, gate_ref = rest[pps + 3:]
    st = pl.program_id(1)
    rows = MOBA_HEADS * dec_seq
    n_pages = pps * n_steps
    n_blocks = n_pages // 2
    lane = lax.broadcasted_iota(jnp.int32, (rows, LANES), 1)
    rowi = lax.broadcasted_iota(jnp.int32, (rows, LANES), 0)

    @pl.when(st == 0)
    def _():
        qt = jnp.concatenate([q_ref[...]] * MOBA_HEADS, axis=0)
        r2 = lax.broadcasted_iota(jnp.int32, qt.shape, 0)
        c2 = lax.broadcasted_iota(jnp.int32, qt.shape, 1)
        qbd = jnp.where(r2 // dec_seq == c2 // HEAD_DIM, qt, 0.0).astype(BF16)
        qbd_ref[...] = qbd
        gate_ref[...] = jnp.zeros(gate_ref.shape, F32)
        kn = jnp.concatenate([kn_ref[...], jnp.zeros((PAGE_SIZE - dec_seq, MOBA_WIDTH), F32)], axis=0)
        sown_ref[...] = _dot_nt(qbd, kn.astype(BF16))

    qbd = qbd_ref[...]
    for i in range(0, pps, 2):
        s0 = _dot(qbd, k_pages[i][...].astype(BF16))
        s1 = _dot(qbd, k_pages[i + 1][...].astype(BF16))
        pg = st * pps + i
        s_ref[pg] = s0
        s_ref[pg + 1] = s1
        mean = jnp.sum(s0 + s1, axis=1, keepdims=True) * (1.0 / MOBA_BLOCK)
        gate_ref[...] = jnp.where(lane == pg // 2, mean, gate_ref[...])

    @pl.when(st == n_steps - 1)
    def _():
        g = gate_ref[...]
        rank = _block_rank(g, lane, n_blocks, n_blocks, 1)
        keep = (lane < n_blocks) & (rank < MOBA_TOPK)
        own_ok = (lane <= rowi % dec_seq) & (lane < dec_seq)
        s_own = jnp.where(own_ok, sown_ref[...], MASKED)
        m_run = s_own
        for n in range(n_blocks):
            kn = keep[:, n:n + 1]
            for pg in (2 * n, 2 * n + 1):
                m_run = jnp.maximum(m_run, jnp.where(kn, s_ref[pg], MASKED))
        m = jnp.max(m_run, axis=1, keepdims=True)
        p_own = jnp.exp(s_own - m)
        pown_ref[...] = p_own.astype(BF16)
        l_run = p_own
        for n in range(n_blocks):
            kn = keep[:, n:n + 1]
            for pg in (2 * n, 2 * n + 1):
                p = jnp.exp(jnp.where(kn, s_ref[pg], MASKED) - m)
                p_ref[pg] = p.astype(BF16)
                l_run = l_run + p
        l_ref[...] = jnp.broadcast_to(jnp.sum(l_run, axis=1, keepdims=True), l_ref.shape)


def _moba_decode_values_kernel(pt_ref, p_ref, pown_ref, l_ref, vn_ref, *rest, pps, n_steps, dec_seq):
    del pt_ref
    v_pages = rest[:pps]
    o_ref, r_ref = rest[pps:]
    st = pl.program_id(1)

    @pl.when(st == 0)
    def _():
        vn = jnp.concatenate([vn_ref[...], jnp.zeros((PAGE_SIZE - dec_seq, MOBA_WIDTH), F32)], axis=0)
        r_ref[...] = _dot(pown_ref[...], vn.astype(BF16))

    acc = r_ref[...]
    for i in range(pps):
        acc = acc + _dot_nt(p_ref[i], v_pages[i][...].astype(BF16))
    r_ref[...] = acc

    @pl.when(st == n_steps - 1)
    def _():
        r = r_ref[...] / l_ref[:, 0:1]
        c2 = lax.broadcasted_iota(jnp.int32, (dec_seq, MOBA_WIDTH), 1)
        out = jnp.zeros((dec_seq, MOBA_WIDTH), F32)
        for h in range(MOBA_HEADS):
            out = jnp.where(c2 // HEAD_DIM == h, r[h * dec_seq:(h + 1) * dec_seq, :], out)
        o_ref[...] = out


def _moba_decode_call(q, k_new, v_new, cache_k, cache_v, page_table, *, page_base, dec_seq, pps):
    n = q.shape[0]
    nb = n // dec_seq
    n_pages = page_table.shape[1]
    assert n_pages % pps == 0 and pps % 2 == 0 and dec_seq % 8 == 0 and dec_seq <= PAGE_SIZE
    n_steps = n_pages // pps
    n_blocks = n_pages // 2
    assert n_blocks <= LANES
    rows = MOBA_HEADS * dec_seq
    base = page_base

    tok = pl.BlockSpec((dec_seq, MOBA_WIDTH), lambda b, st, pt: (b, 0))
    pages = [pl.BlockSpec((None, MOBA_WIDTH, PAGE_SIZE),
                          functools.partial(lambda b, st, pt, i: (base + pt[b, st * pps + i], 0, 0), i=i))
             for i in range(pps)]
    per_batch = lambda b, st, pt: (b, 0, 0)
    stat = pl.BlockSpec((None, rows, LANES), per_batch)
    params = _params(("arbitrary", "arbitrary"))
    probs, p_own, denom = pl.pallas_call(
        functools.partial(_moba_decode_probs_kernel, pps=pps, n_steps=n_steps, dec_seq=dec_seq),
        grid_spec=pltpu.PrefetchScalarGridSpec(
            num_scalar_prefetch=1, grid=(nb, n_steps), in_specs=[tok, tok] + pages,
            out_specs=(pl.BlockSpec((None, n_pages, rows, LANES), lambda b, st, pt: (b, 0, 0, 0)), stat, stat),
            scratch_shapes=[pltpu.VMEM((rows, MOBA_WIDTH), BF16), pltpu.VMEM((n_pages, rows, LANES), F32),
                            pltpu.VMEM((rows, LANES), F32), pltpu.VMEM((rows, LANES), F32)]),
        out_shape=(jax.ShapeDtypeStruct((nb, n_pages, rows, LANES), BF16),
                   jax.ShapeDtypeStruct((nb, rows, LANES), BF16), jax.ShapeDtypeStruct((nb, rows, LANES), F32)),
        compiler_params=params, name="moba_decode_probs",
    )(page_table, q, k_new, *([cache_k] * pps))
    return pl.pallas_call(
        functools.partial(_moba_decode_values_kernel, pps=pps, n_steps=n_steps, dec_seq=dec_seq),
        grid_spec=pltpu.PrefetchScalarGridSpec(
            num_scalar_prefetch=1, grid=(nb, n_steps),
            in_specs=[pl.BlockSpec((None, pps, rows, LANES), lambda b, st, pt: (b, st, 0, 0)), stat, stat, tok]
            + pages,
            out_specs=tok, scratch_shapes=[pltpu.VMEM((rows, MOBA_WIDTH), F32)]),
        out_shape=jax.ShapeDtypeStruct((n, MOBA_WIDTH), F32),
        compiler_params=params, name="moba_decode_values",
    )(page_table, probs, p_own, denom, v_new, *([cache_v] * pps))


def _proj1_kernel(yab_ref, ym_ref, x_ref, wo_ref, g_ref, b_ref, wq_ref, x1_ref, q2_ref, *, alpha, q_scale):
    half = yab_ref.shape[1]
    mix = _dot(yab_ref[...].astype(BF16), wo_ref[0:half, :]) + _dot(ym_ref[...].astype(BF16), wo_ref[half:, :])
    x1 = _layer_norm(alpha * x_ref[...] + mix, g_ref[...], b_ref[...])
    x1_ref[...] = x1
    q2_ref[...] = (_dot(x1.astype(BF16), wq_ref[...]) * q_scale).astype(q2_ref.dtype)


def _proj1_call(yab, ym, x, w_out, g, b, w_q, *, tm, alpha, q_scale, q_dtype):
    n, d = x.shape
    row = lambda i: (i, 0)
    return pl.pallas_call(
        functools.partial(_proj1_kernel, alpha=alpha, q_scale=q_scale),
        grid=(n // tm,),
        in_specs=[pl.BlockSpec((tm, yab.shape[1]), row), pl.BlockSpec((tm, ym.shape[1]), row),
                  pl.BlockSpec((tm, d), row), _resident(w_out.shape), _resident(g.shape), _resident(b.shape),
                  _resident(w_q.shape)],
        out_specs=(pl.BlockSpec((tm, d), row), pl.BlockSpec((tm, d), row)),
        out_shape=(jax.ShapeDtypeStruct((n, d), F32), jax.ShapeDtypeStruct((n, d), q_dtype)),
        compiler_params=_params(("arbitrary",)), name="proj1",
    )(yab, ym, x, w_out, g, b, w_q)


def _xattn_kernel(q_ref, mk_ref, mv_ref, o_ref, *, heads):
    dh = q_ref.shape[1] // heads
    for h in range(heads):
        sl = slice(h * dh, (h + 1) * dh)
        s = _dot_nt(q_ref[:, sl].astype(BF16), mk_ref[:, sl].astype(BF16))
        m = jnp.max(s, axis=1, keepdims=True)
        p = jnp.exp(s - m)
        l = jnp.sum(p, axis=1, keepdims=True)
        o_ref[:, sl] = (_dot(p.astype(BF16), mv_ref[:, sl].astype(BF16)) / l).astype(o_ref.dtype)


def _xattn_cached_kernel(q_ref, mk_ref, mv_ref, o_ref):
    n_mem, heads, dh = mk_ref.shape
    tm = q_ref.shape[0]
    k_all = mk_ref[...].reshape(n_mem * heads, dh).astype(BF16)
    v_all = mv_ref[...].reshape(n_mem * heads, dh).astype(BF16)
    q_rows = jnp.concatenate([q_ref[:, h * dh:(h + 1) * dh] for h in range(heads)], axis=0)
    s = _dot_nt(q_rows.astype(BF16), k_all)
    row_head = lax.broadcasted_iota(jnp.int32, s.shape, 0) // tm
    col_head = lax.broadcasted_iota(jnp.int32, s.shape, 1) % heads
    s = jnp.where(row_head == col_head, s, MASKED)
    m = jnp.max(s, axis=1, keepdims=True)
    p = jnp.exp(s - m)
    l = jnp.sum(p, axis=1, keepdims=True)
    o_rows = _dot(p.astype(BF16), v_all) / l
    for h in range(heads):
        o_ref[:, h * dh:(h + 1) * dh] = o_rows[h * tm:(h + 1) * tm].astype(o_ref.dtype)


def _xattn_call(q2, mem_k, mem_v, k_index, v_index, *, rows_per_batch, tm):
    n, d = q2.shape
    nb = n // rows_per_batch
    tiles = rows_per_batch // tm
    qspec = pl.BlockSpec((tm, d), lambda b, t: (b * tiles + t, 0))
    if mem_k.ndim == 5:
        body = _xattn_cached_kernel
        mem_spec = lambda g: pl.BlockSpec((None, None) + mem_k.shape[2:], lambda b, t: (g, b, 0, 0, 0))
    else:
        body = functools.partial(_xattn_kernel, heads=X_HEADS)
        mem_spec = lambda g: pl.BlockSpec((None, mem_k.shape[1] // nb, d), lambda b, t: (g, b, 0))
    return pl.pallas_call(
        body, grid=(nb, tiles),
        in_specs=[qspec, mem_spec(k_index), mem_spec(v_index)],
        out_specs=qspec, out_shape=jax.ShapeDtypeStruct((n, d), q2.dtype),
        compiler_params=_params(("arbitrary", "arbitrary")), name="xattn",
    )(q2, mem_k, mem_v)


def _proj2_ffn_kernel(o_ref, x1_ref, wo_ref, g2_ref, b2_ref, wg_ref, wu_ref, wd_ref, g3_ref, b3_ref, out_ref,
                      *, alpha, ff_chunk):
    x2 = _layer_norm(alpha * x1_ref[...] + _dot(o_ref[...].astype(BF16), wo_ref[...]), g2_ref[...], b2_ref[...])
    xb = x2.astype(BF16)
    d_ff = wg_ref.shape[1]
    y = jnp.zeros(x2.shape, F32)
    for c in range(d_ff // ff_chunk):
        sl = slice(c * ff_chunk, (c + 1) * ff_chunk)
        gate = _dot(xb, wg_ref[:, sl])
        up = _dot(xb, wu_ref[:, sl])
        hid = gate * (1.0 / (1.0 + jnp.exp(-gate))) * up
        y = y + _dot(hid.astype(BF16), wd_ref[sl, :])
    out_ref[...] = _layer_norm(alpha * x2 + y, g3_ref[...], b3_ref[...])


def _proj2_ffn_call(o, x1, w_o, g2, b2, w_g, w_u, w_d, g3, b3, *, tm, alpha):
    n, d = x1.shape
    row = lambda i: (i, 0)
    consts = (w_o, g2, b2, w_g, w_u, w_d, g3, b3)
    return pl.pallas_call(
        functools.partial(_proj2_ffn_kernel, alpha=alpha, ff_chunk=256),
        grid=(n // tm,),
        in_specs=[pl.BlockSpec((tm, d), row), pl.BlockSpec((tm, d), row)] + [_resident(c.shape) for c in consts],
        out_specs=pl.BlockSpec((tm, d), row), out_shape=jax.ShapeDtypeStruct((n, d), F32),
        compiler_params=_params(("arbitrary",)), name="proj2_ffn",
    )(o, x1, *consts)


def _memkv_kernel(x_ref, w_ref, o_ref):
    o_ref[...] = _dot(x_ref[...].astype(BF16), w_ref[...])


def _memkv_call(mem, w_stack, *, tm):
    n, d = mem.shape
    g = w_stack.shape[0]
    return pl.pallas_call(
        _memkv_kernel, grid=(g, n // tm),
        in_specs=[pl.BlockSpec((tm, d), lambda j, i: (i, 0)), pl.BlockSpec((None, d, d), lambda j, i: (j, 0, 0))],
        out_specs=pl.BlockSpec((None, tm, d), lambda j, i: (j, i, 0)),
        out_shape=jax.ShapeDtypeStruct((g, n, d), F32),
        compiler_params=_params(("arbitrary", "arbitrary")), name="memkv",
    )(mem, w_stack)


def _block_diag(blocks):
    g, c, _ = blocks.shape
    eye = jnp.eye(g, dtype=blocks.dtype)
    return (eye[:, None, :, None] * blocks[:, :, None, :]).reshape(g * c, g * c)


def kernel(x_prompt, x_sample, mem_prompt, state_pool, cache_k, cache_v, cache_mem_k, cache_mem_v, page_table,
           w_in, pool_w, pool_scale, sg_norm_g, sg_norm_b, sg_w, sg_b, w_out, ln1_g, ln1_b, xq_w, xk_w, xv_w,
           xo_w, ln2_g, ln2_b, ffn_gate, ffn_up, ffn_down, ln3_g, ln3_b):
    n_layers = w_in.shape[0]
    bp, seq, d = x_prompt.shape
    bs, dec_seq, _ = x_sample.shape
    n_mem = mem_prompt.shape[1]
    n_phys = cache_k.shape[1]
    past_len = page_table.shape[1] * PAGE_SIZE
    assert past_len % MOBA_BLOCK == 0 and dec_seq <= MOBA_BLOCK and seq % ROW_TILE == 0
    alpha = (2 * n_layers) ** 0.25
    q_scale = (d // X_HEADS) ** -0.5

    xp = x_prompt.reshape(bp * seq, d)
    xs = x_sample.reshape(bs * dec_seq, d)
    paged = lambda c: jnp.transpose(c, (0, 1, 3, 4, 2)).reshape(n_layers * n_phys, MOBA_WIDTH, PAGE_SIZE)
    cache_k2, cache_v2 = paged(cache_k), paged(cache_v)
    hist16 = jnp.pad(state_pool, ((0, 0), (0, 0), (1, 0), (0, 0)))

    vec = lambda p: p.reshape(n_layers, 1, -1)
    w_in_b, w_out_b, xq_b, xo_b = (w.astype(BF16) for w in (w_in, w_out, xq_w, xo_w))
    wg_b, wu_b, wd_b = (w.astype(BF16) for w in (ffn_gate, ffn_up, ffn_down))
    pw_bd = jnp.stack([_block_diag(pool_w[l]) for l in range(n_layers)]).astype(BF16)
    causal = jnp.tril(jnp.ones((SG_CHUNK, SG_CHUNK), bool))
    sgw_p = jnp.where(causal, sg_w, 0.0).astype(BF16)
    sgb_p = jnp.repeat(jnp.swapaxes(sg_b, 1, 2), HEAD_DIM, axis=2)
    eye_s = jnp.eye(bs, dtype=F32)
    w_dec = jnp.where(causal[:dec_seq, :dec_seq], sg_w[:, :, :dec_seq, :dec_seq], 0.0)
    sgw_s = (eye_s[None, None, :, None, :, None] * w_dec[:, :, None, :, None, :]).reshape(
        n_layers, SG_HEADS, bs * dec_seq, bs * dec_seq).astype(BF16)
    sgb_s = jnp.tile(sgb_p[:, :dec_seq, :], (1, bs, 1))
    kv_stack = jnp.concatenate([xk_w, xv_w], axis=0).astype(BF16)

    memkv = _memkv_call(mem_prompt.reshape(bp * n_mem, d), kv_stack, tm=ROW_TILE)

    pool_p, pool_s, ks_l, vs_l, sgv_l = [], [], [], [], []
    kv_new = None
    for l in range(n_layers):
        ln = lambda p: vec(p)[l]
        yab, q, kt, vt, kb, hist = _in_prompt_call(
            xp, w_in_b[l], pw_bd[l], ln(pool_scale), ln(sg_norm_g), ln(sg_norm_b), sgw_p[l], sgb_p[l],
            kv_new, seq=seq, tm=IN_ROW_TILE, layer=l, n_layers=n_layers)
        kv_new = (kt, vt)
        ym = _moba_prompt_call(q, kb, vt, seq=seq, pairs_per_step=MOBA_PAIRS_PER_STEP, layer=l)
        x1, q2 = _proj1_call(yab, ym, xp, w_out_b[l], ln(ln1_g), ln(ln1_b), xq_b[l], tm=IN_ROW_TILE, alpha=alpha,
                             q_scale=q_scale, q_dtype=BF16)
        o = _xattn_call(q2, memkv, memkv, l, n_layers + l, rows_per_batch=seq, tm=ROW_TILE)
        xp = _proj2_ffn_call(o, x1, xo_b[l], ln(ln2_g), ln(ln2_b), wg_b[l], wu_b[l], wd_b[l], ln(ln3_g),
                             ln(ln3_b), tm=ROW_TILE, alpha=alpha)
        pool_p.append(hist[:, 1:, :])
        yab, q, k, v, hist, sgv = _in_sample_call(
            xs, hist16[l], w_in_b[l], pw_bd[l], ln(pool_scale), ln(sg_norm_g), ln(sg_norm_b), sgw_s[l], sgb_s[l],
            dec_seq=dec_seq, pos0=past_len)
        ym = _moba_decode_call(q, k, v, cache_k2, cache_v2, page_table, page_base=l * n_phys, dec_seq=dec_seq,
                               pps=DECODE_PAGES_PER_STEP)
        x1, q2 = _proj1_call(yab, ym, xs, w_out_b[l], ln(ln1_g), ln(ln1_b), xq_b[l], tm=bs * dec_seq, alpha=alpha,
                             q_scale=q_scale, q_dtype=F32)
        o = _xattn_call(q2, cache_mem_k, cache_mem_v, l, l, rows_per_batch=dec_seq, tm=dec_seq)
        xs = _proj2_ffn_call(o, x1, xo_b[l], ln(ln2_g), ln(ln2_b), wg_b[l], wu_b[l], wd_b[l], ln(ln3_g),
                             ln(ln3_b), tm=bs * dec_seq, alpha=alpha)
        pool_s.append(hist[:, 1:, :]); ks_l.append(k); vs_l.append(v); sgv_l.append(sgv)

    heads = lambda a, nb_, s_: a.reshape(nb_, s_, MOBA_HEADS, HEAD_DIM)
    tok_major = lambda a: jnp.transpose(a, (0, 1, 4, 2, 3))
    mem_shape = (n_layers, bp, n_mem, X_HEADS, d // X_HEADS)
    return (xp.reshape(bp, seq, d), xs.reshape(bs, dec_seq, d),
            jnp.stack(pool_p), jnp.stack(pool_s),
            tok_major(kv_new[0]), tok_major(kv_new[1]),
            jnp.stack([heads(a, bs, dec_seq) for a in ks_l]), jnp.stack([heads(a, bs, dec_seq) for a in vs_l]),
            jnp.stack([a.reshape(bs, dec_seq, SG_WIDTH) for a in sgv_l]),
            memkv[:n_layers].reshape(mem_shape), memkv[n_layers:].reshape(mem_shape))
```

```python
import functools

import jax
import jax.numpy as jnp
from jax import lax
from jax.experimental import pallas as pl
from jax.experimental.pallas import tpu as pltpu

F32 = jnp.float32
BF16 = jnp.bfloat16

HEAD_DIM = 64
POOL_WIDTH = 256
POOL_WINDOWS = (2, 4, 8, 16)
POOL_GROUP_DIM = POOL_WIDTH // len(POOL_WINDOWS)
POOL_HIST = max(POOL_WINDOWS) - 1
HIST_ROWS = POOL_HIST + 1
SG_WIDTH = 256
SG_HEADS = SG_WIDTH // HEAD_DIM
SG_CHUNK = 128
MOBA_WIDTH = 512
MOBA_HEADS = MOBA_WIDTH // HEAD_DIM
MOBA_BLOCK = 256
MOBA_TOPK = 3
PAGE_SIZE = 128
X_HEADS = 4
LN_EPS = 1e-5
MASKED = -1e30

LANES = 128
VMEM_LIMIT_BYTES = 56 * 1024 * 1024
ROW_TILE = 512
DECODE_PAGES_PER_STEP = 64
IN_ROW_TILE = 1024
MOBA_PAIRS_PER_STEP = 2


def _dot(a, b):
    return jnp.dot(a, b, preferred_element_type=F32)


def _dot_nt(a, b, precision=None):
    return lax.dot_general(a, b, (((1,), (1,)), ((), ())), preferred_element_type=F32, precision=precision)


def _layer_norm(y, g, b):
    mu = jnp.mean(y, axis=-1, keepdims=True)
    d = y - mu
    var = jnp.mean(d * d, axis=-1, keepdims=True)
    return d * lax.rsqrt(var + LN_EPS) * g + b


def _params(sem):
    return pltpu.CompilerParams(dimension_semantics=sem, vmem_limit_bytes=VMEM_LIMIT_BYTES)


def _resident(shape):
    nd = len(shape)
    return pl.BlockSpec(shape, lambda *_: (0,) * nd, pipeline_mode=pl.Buffered(1))


def _window_sums(ext):
    s2 = ext + pltpu.roll(ext, 1, 0)
    s4 = s2 + pltpu.roll(s2, 2, 0)
    s8 = s4 + pltpu.roll(s4, 4, 0)
    s16 = s8 + pltpu.roll(s8, 8, 0)
    lane = lax.broadcasted_iota(jnp.int32, ext.shape, 1)
    g = POOL_GROUP_DIM
    return jnp.where(lane < g, s2, jnp.where(lane < 2 * g, s4, jnp.where(lane < 3 * g, s8, s16)))


def _pool_out(sums, a, pos, pw_bd, scale):
    lane = lax.broadcasted_iota(jnp.int32, a.shape, 1)
    g = POOL_GROUP_DIM
    win = jnp.where(lane < g, 2, jnp.where(lane < 2 * g, 4, jnp.where(lane < 3 * g, 8, 16)))
    cnt = jnp.minimum(win, pos + 1).astype(F32)
    pooled = sums / cnt - a
    return _dot(pooled.astype(BF16), pw_bd) * scale


def _spatial_gate(zb, n_g, n_b, sgw_ref, bias, chunk):
    z = jax.nn.gelu(zb, approximate=True)
    u = z[:, :SG_WIDTH]
    v = _layer_norm(z[:, SG_WIDTH:], n_g, n_b)
    vb = v.astype(BF16)
    lane = lax.broadcasted_iota(jnp.int32, (chunk, LANES), 1)
    rows = []
    for c in range(zb.shape[0] // chunk):
        vc = vb[c * chunk:(c + 1) * chunk]
        cols = []
        for pair in range(SG_HEADS // 2):
            vp = vc[:, pair * LANES:(pair + 1) * LANES]
            s0 = _dot(sgw_ref[2 * pair], vp)
            s1 = _dot(sgw_ref[2 * pair + 1], vp)
            cols.append(jnp.where(lane < HEAD_DIM, s0, s1))
        rows.append(jnp.concatenate(cols, axis=1) + bias)
    s = rows[0] if len(rows) == 1 else jnp.concatenate(rows, axis=0)
    return u * s, v


def _project_qkv(xb, w_ref, q_ref, k_ref, v_ref, kb_ref, transposed):
    c0 = POOL_WIDTH + 2 * SG_WIDTH
    q = _dot(xb, w_ref[:, c0:c0 + MOBA_WIDTH]) * (HEAD_DIM ** -0.5)
    q_ref[...] = q.astype(q_ref.dtype)
    k = _dot(xb, w_ref[:, c0 + MOBA_WIDTH:c0 + 2 * MOBA_WIDTH])
    v = _dot(xb, w_ref[:, c0 + 2 * MOBA_WIDTH:c0 + 3 * MOBA_WIDTH])
    if transposed:
        k_ref[...] = k.T.reshape(k_ref.shape)
        v_ref[...] = v.T.reshape(v_ref.shape)
        kb_ref[...] = k.astype(BF16)
    else:
        k_ref[...] = k
        v_ref[...] = v


def _in_prompt_kernel(x_ref, w_ref, pw_ref, ps_ref, ng_ref, nb_ref, sgw_ref, sgb_ref, *rest, tiles_per_seq):
    yab_ref, q_ref, kt_ref, vt_ref, kb_ref, hist_ref, ext_ref = rest[-7:]
    tm = x_ref.shape[0]
    t = pl.program_id(0) % tiles_per_seq

    @pl.when(t == 0)
    def _():
        ext_ref[0:HIST_ROWS, :] = jnp.zeros((HIST_ROWS, POOL_WIDTH), F32)

    xb = x_ref[...].astype(BF16)
    a = _dot(xb, w_ref[:, 0:POOL_WIDTH])
    ext_ref[HIST_ROWS:, :] = a
    sums = _window_sums(ext_ref[...])[HIST_ROWS:]
    pos = t * tm + lax.broadcasted_iota(jnp.int32, a.shape, 0)
    yab_ref[:, 0:POOL_WIDTH] = _pool_out(sums, a, pos, pw_ref[...], ps_ref[...]).astype(yab_ref.dtype)
    tail = ext_ref[tm:tm + HIST_ROWS, :]
    hist_ref[...] = tail
    ext_ref[0:HIST_ROWS, :] = tail

    zb = _dot(xb, w_ref[:, POOL_WIDTH:POOL_WIDTH + 2 * SG_WIDTH])
    y_sg, _ = _spatial_gate(zb, ng_ref[...], nb_ref[...], sgw_ref, sgb_ref[...], SG_CHUNK)
    yab_ref[:, POOL_WIDTH:] = y_sg.astype(yab_ref.dtype)
    _project_qkv(xb, w_ref, q_ref, kt_ref, vt_ref, kb_ref, True)


def _in_prompt_call(x, w_in, pw_bd, pscale, n_g, n_b, sgw, sgb, kv_buffers, *, seq, tm, layer, n_layers):
    n, d = x.shape
    nb = n // seq
    tiles_per_seq = seq // tm
    row = lambda i: (i, 0)
    tok_minor = lambda i: (layer, i // tiles_per_seq, 0, 0, i % tiles_per_seq)
    kv_shape = jax.ShapeDtypeStruct((n_layers, nb, MOBA_HEADS, HEAD_DIM, seq), F32)
    out_shape = (
        jax.ShapeDtypeStruct((n, POOL_WIDTH + SG_WIDTH), BF16),
        jax.ShapeDtypeStruct((n, MOBA_WIDTH), BF16),
        kv_shape,
        kv_shape,
        jax.ShapeDtypeStruct((n, MOBA_WIDTH), BF16),
        jax.ShapeDtypeStruct((nb, HIST_ROWS, POOL_WIDTH), F32),
    )
    out_specs = (
        pl.BlockSpec((tm, POOL_WIDTH + SG_WIDTH), row),
        pl.BlockSpec((tm, MOBA_WIDTH), row),
        pl.BlockSpec((None, None, MOBA_HEADS, HEAD_DIM, tm), tok_minor),
        pl.BlockSpec((None, None, MOBA_HEADS, HEAD_DIM, tm), tok_minor),
        pl.BlockSpec((tm, MOBA_WIDTH), row),
        pl.BlockSpec((None, HIST_ROWS, POOL_WIDTH), lambda i: (i // tiles_per_seq, 0, 0)),
    )
    args = [x, w_in, pw_bd, pscale, n_g, n_b, sgw, sgb]
    in_specs = [pl.BlockSpec((tm, d), row)] + [_resident(a.shape) for a in args[1:]]
    aliases = {}
    if kv_buffers is not None:
        aliases = {len(args): 2, len(args) + 1: 3}
        args += list(kv_buffers)
        in_specs += [pl.BlockSpec(memory_space=pl.ANY)] * 2
    return pl.pallas_call(
        functools.partial(_in_prompt_kernel, tiles_per_seq=tiles_per_seq),
        grid=(n // tm,), in_specs=in_specs, out_specs=out_specs, out_shape=out_shape,
        input_output_aliases=aliases,
        scratch_shapes=[pltpu.VMEM((HIST_ROWS + tm, POOL_WIDTH), F32)],
        compiler_params=_params(("arbitrary",)), name="in_prompt",
    )(*args)


def _in_sample_kernel(x_ref, hist_in_ref, w_ref, pw_ref, ps_ref, ng_ref, nb_ref, sgw_ref, sgb_ref,
                      yab_ref, q_ref, k_ref, v_ref, hist_ref, sgv_ref, *, dec_seq, pos0):
    n = x_ref.shape[0]
    nb = n // dec_seq
    seg = HIST_ROWS + dec_seq
    xb = x_ref[...].astype(BF16)
    a = _dot(xb, w_ref[:, 0:POOL_WIDTH])
    ext = jnp.concatenate([hist_in_ref[...], a.reshape(nb, dec_seq, POOL_WIDTH)], axis=1)
    sums = _window_sums(ext.reshape(nb * seg, POOL_WIDTH)).reshape(nb, seg, POOL_WIDTH)
    sums = sums[:, HIST_ROWS:, :].reshape(n, POOL_WIDTH)
    pos = pos0 + lax.broadcasted_iota(jnp.int32, a.shape, 0) % dec_seq
    yab_ref[:, 0:POOL_WIDTH] = _pool_out(sums, a, pos, pw_ref[...], ps_ref[...]).astype(yab_ref.dtype)
    hist_ref[...] = ext[:, seg - HIST_ROWS:, :]

    zb = _dot(xb, w_ref[:, POOL_WIDTH:POOL_WIDTH + 2 * SG_WIDTH])
    y_sg, v_n = _spatial_gate(zb, ng_ref[...], nb_ref[...], sgw_ref, sgb_ref[...], n)
    yab_ref[:, POOL_WIDTH:] = y_sg.astype(yab_ref.dtype)
    sgv_ref[...] = v_n
    _project_qkv(xb, w_ref, q_ref, k_ref, v_ref, None, False)


def _in_sample_call(x, hist16, w_in, pw_bd, pscale, n_g, n_b, sgw_bd, sgb_t, *, dec_seq, pos0):
    n, d = x.shape
    nb = n // dec_seq
    out_shape = (
        jax.ShapeDtypeStruct((n, POOL_WIDTH + SG_WIDTH), BF16),
        jax.ShapeDtypeStruct((n, MOBA_WIDTH), F32),
        jax.ShapeDtypeStruct((n, MOBA_WIDTH), F32),
        jax.ShapeDtypeStruct((n, MOBA_WIDTH), F32),
        jax.ShapeDtypeStruct((nb, HIST_ROWS, POOL_WIDTH), F32),
        jax.ShapeDtypeStruct((n, SG_WIDTH), F32),
    )
    args = (x, hist16, w_in, pw_bd, pscale, n_g, n_b, sgw_bd, sgb_t)
    return pl.pallas_call(
        functools.partial(_in_sample_kernel, dec_seq=dec_seq, pos0=pos0),
        grid=(1,), in_specs=[_resident(a.shape) for a in args],
        out_specs=tuple(_resident(s.shape) for s in out_shape), out_shape=out_shape,
        compiler_params=_params(("arbitrary",)), name="in_sample",
    )(*args)


def _block_rank(g, blk, n_valid, n_blocks, axis):
    rank = jnp.zeros(g.shape, jnp.int32)
    for m in range(n_blocks):
        gm = lax.slice_in_dim(g, m, m + 1, axis=axis)
        beats = (gm > g) | ((gm == g) & (m < blk))
        rank = rank + jnp.where(beats & (m < n_valid), 1, 0)
    return rank


GATE_ROWS = 16


def _block_penalty(g, blk, n_valid):
    neg = -jnp.inf
    g = jnp.where(blk < n_valid, g, neg)
    pen = jnp.full(g.shape, MASKED, F32)
    for _ in range(MOBA_TOPK):
        mx = jnp.max(g, axis=0, keepdims=True)
        first = jnp.min(jnp.where(g == mx, blk, GATE_ROWS), axis=0, keepdims=True)
        pick = (blk == first) & (mx > neg)
        pen = jnp.where(pick, 0.0, pen)
        g = jnp.where(pick, neg, g)
    return pen


def _moba_prompt_kernel(q_ref, k_ref, vt_ref, o_ref, kaug_ref, vaug_ref, kmean_ref, qaug_ref, s_ref, smax_ref, m_ref,
                        acc_ref, *, n_blocks):
    blk_rows = MOBA_BLOCK
    qi = pl.program_id(2)
    n_pair = kaug_ref.shape[0]
    n_head = 2 * n_pair

    @pl.when(qi == 0)
    def _():
        lane = lax.broadcasted_iota(jnp.int32, (blk_rows, LANES), 1)
        sub = lax.broadcasted_iota(jnp.int32, (LANES, blk_rows), 0)
        lane1 = lax.broadcasted_iota(jnp.int32, (1, LANES), 1)
        kmean_ref[...] = jnp.zeros(kmean_ref.shape, F32)
        for hp in range(n_pair):
            for n in range(n_blocks):
                kb = k_ref[n * blk_rows:(n + 1) * blk_rows, hp * LANES:(hp + 1) * LANES]
                kaug_ref[hp, n * blk_rows:(n + 1) * blk_rows, 0:LANES] = kb
                kaug_ref[hp, n * blk_rows:(n + 1) * blk_rows, LANES:] = jnp.where(lane == n, 1.0, 0.0).astype(BF16)
                km = jnp.mean(kb.astype(F32), axis=0, keepdims=True)
                kmean_ref[hp, n:n + 1, :] = jnp.where(lane1 < HEAD_DIM, km, 0.0)
                kmean_ref[hp, GATE_ROWS + n:GATE_ROWS + n + 1, :] = jnp.where(lane1 < HEAD_DIM, 0.0, km)
                vt = vt_ref[2 * hp:2 * hp + 2, :, n * blk_rows:(n + 1) * blk_rows].reshape(LANES, blk_rows)
                vaug_ref[n, 2 * hp] = jnp.where(sub < HEAD_DIM, vt, 1.0).astype(BF16)
                vaug_ref[n, 2 * hp + 1] = jnp.where(sub >= HEAD_DIM, vt, 1.0).astype(BF16)
        seq = q_ref.shape[0]
        lane_q = lax.broadcasted_iota(jnp.int32, (seq, LANES), 1)
        blk = lax.broadcasted_iota(jnp.int32, (GATE_ROWS, seq), 0)
        own_blk = lax.broadcasted_iota(jnp.int32, (GATE_ROWS, seq), 1) // blk_rows
        for hp in range(n_pair):
            q = q_ref[:, hp * LANES:(hp + 1) * LANES]
            gate_t = _dot_nt(kmean_ref[hp], q.astype(F32), precision=lax.Precision.HIGHEST)
            for h in range(2):
                qh = jnp.where(lane_q // HEAD_DIM == h, q, jnp.zeros_like(q))
                pen_t = _block_penalty(gate_t[h * GATE_ROWS:(h + 1) * GATE_ROWS], blk, own_blk)
                pen_t = jnp.where(blk == own_blk, 0.0, pen_t)
                pen_t = jnp.concatenate([pen_t, jnp.zeros((LANES - GATE_ROWS, seq), F32)], axis=0)
                for n in range(n_blocks):
                    rows = slice(n * blk_rows, (n + 1) * blk_rows)
                    qaug_ref[n, 2 * hp + h] = jnp.concatenate(
                        [qh[rows], pen_t[:, rows].T.astype(BF16)], axis=1)

    def scores(t, slot):
        off = pl.multiple_of(t * 2 * blk_rows, 2 * blk_rows)
        for hp in range(n_pair):
            kj = kaug_ref[hp, pl.ds(off, 2 * blk_rows), :]
            for g in (2 * hp, 2 * hp + 1):
                st = _dot_nt(kj, qaug_ref[qi, g])
                s_ref[slot, g] = st
                smax_ref[slot, g] = jnp.max(st, axis=0, keepdims=True)

    def accumulate(t, slot):
        for g in range(n_head):
            st = s_ref[slot, g]
            m_prev = m_ref[g]
            m_new = jnp.maximum(m_prev, smax_ref[slot, g])
            p = jnp.exp(st - m_new).astype(BF16)
            pv = _dot(vaug_ref[2 * t, g], p[0:blk_rows]) + _dot(vaug_ref[2 * t + 1, g], p[blk_rows:])
            acc_ref[g] = jnp.exp(m_prev - m_new) * acc_ref[g] + pv
            m_ref[g] = m_new

    own = qi // 2
    scores(own, 0)
    scores(jnp.maximum(own - 1, 0), 1)
    key_i = lax.broadcasted_iota(jnp.int32, (2 * blk_rows, blk_rows), 0) - (qi % 2) * blk_rows
    qry_i = lax.broadcasted_iota(jnp.int32, (2 * blk_rows, blk_rows), 1)
    for g in range(n_head):
        st = jnp.where(key_i <= qry_i, s_ref[0, g], MASKED)
        m = jnp.max(st, axis=0, keepdims=True)
        p = jnp.exp(st - m).astype(BF16)
        acc_ref[g] = _dot(vaug_ref[2 * own, g], p[0:blk_rows]) + _dot(vaug_ref[2 * own + 1, g], p[blk_rows:])
        m_ref[g] = m

    def body(j, carry):
        scores(own - 2 - j, j % 2)
        accumulate(own - 1 - j, (1 + j) % 2)
        return carry

    lax.fori_loop(0, own - 1, body, 0)

    @pl.when(own > 0)
    def _():
        accumulate(0, own % 2)

    for hp in range(n_pair):
        a0 = acc_ref[2 * hp]
        a1 = acc_ref[2 * hp + 1]
        out_t = jnp.concatenate([a0[0:HEAD_DIM] / a0[HEAD_DIM:], a1[HEAD_DIM:] / a1[0:HEAD_DIM]], axis=0)
        o_ref[:, hp * LANES:(hp + 1) * LANES] = out_t.T.astype(o_ref.dtype)


def _moba_prompt_call(q, kb, vt, *, seq, pairs_per_step, layer):
    n = q.shape[0]
    nb = n // seq
    n_blocks = seq // MOBA_BLOCK
    assert n_blocks <= GATE_ROWS and n_blocks % 2 == 0
    steps = MOBA_WIDTH // (LANES * pairs_per_step)
    width = LANES * pairs_per_step
    heads = 2 * pairs_per_step
    ospec = pl.BlockSpec((MOBA_BLOCK, width), lambda b, hp, i: (b * n_blocks + i, hp))
    kspec = pl.BlockSpec((seq, width), lambda b, hp, i: (b, hp))
    vspec = pl.BlockSpec((None, None, heads, HEAD_DIM, seq), lambda b, hp, i: (layer, b, hp, 0, 0))
    return pl.pallas_call(
        functools.partial(_moba_prompt_kernel, n_blocks=n_blocks),
        grid=(nb, steps, n_blocks), in_specs=[kspec, kspec, vspec], out_specs=ospec,
        out_shape=jax.ShapeDtypeStruct((n, MOBA_WIDTH), BF16),
        scratch_shapes=[pltpu.VMEM((pairs_per_step, seq, 2 * LANES), BF16),
                        pltpu.VMEM((n_blocks, heads, LANES, MOBA_BLOCK), BF16),
                        pltpu.VMEM((pairs_per_step, 2 * GATE_ROWS, LANES), F32),
                        pltpu.VMEM((n_blocks, heads, MOBA_BLOCK, 2 * LANES), BF16),
                        pltpu.VMEM((2, heads, 2 * MOBA_BLOCK, MOBA_BLOCK), F32),
                        pltpu.VMEM((2, heads, 1, MOBA_BLOCK), F32),
                        pltpu.VMEM((heads, 1, MOBA_BLOCK), F32),
                        pltpu.VMEM((heads, LANES, MOBA_BLOCK), F32)],
        compiler_params=_params(("arbitrary", "arbitrary", "arbitrary")), name="moba_prompt",
    )(q, kb, vt)


def _moba_decode_probs_kernel(pt_ref, q_ref, kn_ref, *rest, pps, n_steps, dec_seq):
    del pt_ref
    k_pages = rest[:pps]
    p_ref, pown_ref, l_ref = rest[pps:pps + 3]
    qbd_ref, s_ref, sown_ref, gate_ref = rest[pps + 3:]
    st = pl.program_id(1)
    rows = MOBA_HEADS * dec_seq
    n_pages = pps * n_steps
    n_blocks = n_pages // 2
    lane = lax.broadcasted_iota(jnp.int32, (rows, LANES), 1)
    rowi = lax.broadcasted_iota(jnp.int32, (rows, LANES), 0)

    @pl.when(st == 0)
    def _():
        qt = jnp.concatenate([q_ref[...]] * MOBA_HEADS, axis=0)
        r2 = lax.broadcasted_iota(jnp.int32, qt.shape, 0)
        c2 = lax.broadcasted_iota(jnp.int32, qt.shape, 1)
        qbd = jnp.where(r2 // dec_seq == c2 // HEAD_DIM, qt, 0.0).astype(BF16)
        qbd_ref[...] = qbd
        gate_ref[...] = jnp.zeros(gate_ref.shape, F32)
        kn = jnp.concatenate([kn_ref[...], jnp.zeros((PAGE_SIZE - dec_seq, MOBA_WIDTH), F32)], axis=0)
        sown_ref[...] = _dot_nt(qbd, kn.astype(BF16))

    qbd = qbd_ref[...]
    for i in range(0, pps, 2):
        s0 = _dot(qbd, k_pages[i][...].astype(BF16))
        s1 = _dot(qbd, k_pages[i + 1][...].astype(BF16))
        pg = st * pps + i
        s_ref[pg] = s0
        s_ref[pg + 1] = s1
        mean = jnp.sum(s0 + s1, axis=1, keepdims=True) * (1.0 / MOBA_BLOCK)
        gate_ref[...] = jnp.where(lane == pg // 2, mean, gate_ref[...])

    @pl.when(st == n_steps - 1)
    def _():
        g = gate_ref[...]
        rank = _block_rank(g, lane, n_blocks, n_blocks, 1)
        keep = (lane < n_blocks) & (rank < MOBA_TOPK)
        own_ok = (lane <= rowi % dec_seq) & (lane < dec_seq)
        s_own = jnp.where(own_ok, sown_ref[...], MASKED)
        m_run = s_own
        for n in range(n_blocks):
            kn = keep[:, n:n + 1]
            for pg in (2 * n, 2 * n + 1):
                m_run = jnp.maximum(m_run, jnp.where(kn, s_ref[pg], MASKED))
        m = jnp.max(m_run, axis=1, keepdims=True)
        p_own = jnp.exp(s_own - m)
        pown_ref[...] = p_own.astype(BF16)
        l_run = p_own
        for n in range(n_blocks):
            kn = keep[:, n:n + 1]
            for pg in (2 * n, 2 * n + 1):
                p = jnp.exp(jnp.where(kn, s_ref[pg], MASKED) - m)
                p_ref[pg] = p.astype(BF16)
                l_run = l_run + p
        l_ref[...] = jnp.broadcast_to(jnp.sum(l_run, axis=1, keepdims=True), l_ref.shape)


def _moba_decode_values_kernel(pt_ref, p_ref, pown_ref, l_ref, vn_ref, *rest, pps, n_steps, dec_seq):
    del pt_ref
    v_pages = rest[:pps]
    o_ref, r_ref = rest[pps:]
    st = pl.program_id(1)

    @pl.when(st == 0)
    def _():
        vn = jnp.concatenate([vn_ref[...], jnp.zeros((PAGE_SIZE - dec_seq, MOBA_WIDTH), F32)], axis=0)
        r_ref[...] = _dot(pown_ref[...], vn.astype(BF16))

    acc = r_ref[...]
    for i in range(pps):
        acc = acc + _dot_nt(p_ref[i], v_pages[i][...].astype(BF16))
    r_ref[...] = acc

    @pl.when(st == n_steps - 1)
    def _():
        r = r_ref[...] / l_ref[:, 0:1]
        c2 = lax.broadcasted_iota(jnp.int32, (dec_seq, MOBA_WIDTH), 1)
        out = jnp.zeros((dec_seq, MOBA_WIDTH), F32)
        for h in range(MOBA_HEADS):
            out = jnp.where(c2 // HEAD_DIM == h, r[h * dec_seq:(h + 1) * dec_seq, :], out)
        o_ref[...] = out


def _moba_decode_call(q, k_new, v_new, cache_k, cache_v, page_table, *, page_base, dec_seq, pps):
    n = q.shape[0]
    nb = n // dec_seq
    n_pages = page_table.shape[1]
    assert n_pages % pps == 0 and pps % 2 == 0 and dec_seq % 8 == 0 and dec_seq <= PAGE_SIZE
    n_steps = n_pages // pps
    n_blocks = n_pages // 2
    assert n_blocks <= LANES
    rows = MOBA_HEADS * dec_seq
    base = page_base

    tok = pl.BlockSpec((dec_seq, MOBA_WIDTH), lambda b, st, pt: (b, 0))
    pages = [pl.BlockSpec((None, MOBA_WIDTH, PAGE_SIZE),
                          functools.partial(lambda b, st, pt, i: (base + pt[b, st * pps + i], 0, 0), i=i))
             for i in range(pps)]
    per_batch = lambda b, st, pt: (b, 0, 0)
    stat = pl.BlockSpec((None, rows, LANES), per_batch)
    params = _params(("arbitrary", "arbitrary"))
    probs, p_own, denom = pl.pallas_call(
        functools.partial(_moba_decode_probs_kernel, pps=pps, n_steps=n_steps, dec_seq=dec_seq),
        grid_spec=pltpu.PrefetchScalarGridSpec(
            num_scalar_prefetch=1, grid=(nb, n_steps), in_specs=[tok, tok] + pages,
            out_specs=(pl.BlockSpec((None, n_pages, rows, LANES), lambda b, st, pt: (b, 0, 0, 0)), stat, stat),
            scratch_shapes=[pltpu.VMEM((rows, MOBA_WIDTH), BF16), pltpu.VMEM((n_pages, rows, LANES), F32),
                            pltpu.VMEM((rows, LANES), F32), pltpu.VMEM((rows, LANES), F32)]),
        out_shape=(jax.ShapeDtypeStruct((nb, n_pages, rows, LANES), BF16),
                   jax.ShapeDtypeStruct((nb, rows, LANES), BF16), jax.ShapeDtypeStruct((nb, rows, LANES), F32)),
        compiler_params=params, name="moba_decode_probs",
    )(page_table, q, k_new, *([cache_k] * pps))
    return pl.pallas_call(
        functools.partial(_moba_decode_values_kernel, pps=pps, n_steps=n_steps, dec_seq=dec_seq),
        grid_spec=pltpu.PrefetchScalarGridSpec(
            num_scalar_prefetch=1, grid=(nb, n_steps),
            in_specs=[pl.BlockSpec((None, pps, rows, LANES), lambda b, st, pt: (b, st, 0, 0)), stat, stat, tok]
            + pages,
            out_specs=tok, scratch_shapes=[pltpu.VMEM((rows, MOBA_WIDTH), F32)]),
        out_shape=jax.ShapeDtypeStruct((n, MOBA_WIDTH), F32),
        compiler_params=params, name="moba_decode_values",
    )(page_table, probs, p_own, denom, v_new, *([cache_v] * pps))


def _proj1_kernel(yab_ref, ym_ref, x_ref, wo_ref, g_ref, b_ref, wq_ref, x1_ref, q2_ref, *, alpha, q_scale):
    half = yab_ref.shape[1]
    mix = _dot(yab_ref[...].astype(BF16), wo_ref[0:half, :]) + _dot(ym_ref[...].astype(BF16), wo_ref[half:, :])
    x1 = _layer_norm(alpha * x_ref[...] + mix, g_ref[...], b_ref[...])
    x1_ref[...] = x1
    q2_ref[...] = (_dot(x1.astype(BF16), wq_ref[...]) * q_scale).astype(q2_ref.dtype)


def _proj1_call(yab, ym, x, w_out, g, b, w_q, *, tm, alpha, q_scale, q_dtype):
    n, d = x.shape
    row = lambda i: (i, 0)
    return pl.pallas_call(
        functools.partial(_proj1_kernel, alpha=alpha, q_scale=q_scale),
        grid=(n // tm,),
        in_specs=[pl.BlockSpec((tm, yab.shape[1]), row), pl.BlockSpec((tm, ym.shape[1]), row),
                  pl.BlockSpec((tm, d), row), _resident(w_out.shape), _resident(g.shape), _resident(b.shape),
                  _resident(w_q.shape)],
        out_specs=(pl.BlockSpec((tm, d), row), pl.BlockSpec((tm, d), row)),
        out_shape=(jax.ShapeDtypeStruct((n, d), F32), jax.ShapeDtypeStruct((n, d), q_dtype)),
        compiler_params=_params(("arbitrary",)), name="proj1",
    )(yab, ym, x, w_out, g, b, w_q)


def _xattn_kernel(q_ref, mk_ref, mv_ref, o_ref, *, heads):
    dh = q_ref.shape[1] // heads
    for h in range(heads):
        sl = slice(h * dh, (h + 1) * dh)
        s = _dot_nt(q_ref[:, sl].astype(BF16), mk_ref[:, sl].astype(BF16))
        m = jnp.max(s, axis=1, keepdims=True)
        p = jnp.exp(s - m)
        l = jnp.sum(p, axis=1, keepdims=True)
        o_ref[:, sl] = (_dot(p.astype(BF16), mv_ref[:, sl].astype(BF16)) / l).astype(o_ref.dtype)


def _xattn_cached_kernel(q_ref, mk_ref, mv_ref, o_ref):
    n_mem, heads, dh = mk_ref.shape
    tm = q_ref.shape[0]
    k_all = mk_ref[...].reshape(n_mem * heads, dh).astype(BF16)
    v_all = mv_ref[...].reshape(n_mem * heads, dh).astype(BF16)
    q_rows = jnp.concatenate([q_ref[:, h * dh:(h + 1) * dh] for h in range(heads)], axis=0)
    s = _dot_nt(q_rows.astype(BF16), k_all)
    row_head = lax.broadcasted_iota(jnp.int32, s.shape, 0) // tm
    col_head = lax.broadcasted_iota(jnp.int32, s.shape, 1) % heads
    s = jnp.where(row_head == col_head, s, MASKED)
    m = jnp.max(s, axis=1, keepdims=True)
    p = jnp.exp(s - m)
    l = jnp.sum(p, axis=1, keepdims=True)
    o_rows = _dot(p.astype(BF16), v_all) / l
    for h in range(heads):
        o_ref[:, h * dh:(h + 1) * dh] = o_rows[h * tm:(h + 1) * tm].astype(o_ref.dtype)


def _xattn_call(q2, mem_k, mem_v, k_index, v_index, *, rows_per_batch, tm):
    n, d = q2.shape
    nb = n // rows_per_batch
    tiles = rows_per_batch // tm
    qspec = pl.BlockSpec((tm, d), lambda b, t: (b * tiles + t, 0))
    if mem_k.ndim == 5:
        body = _xattn_cached_kernel
        mem_spec = lambda g: pl.BlockSpec((None, None) + mem_k.shape[2:], lambda b, t: (g, b, 0, 0, 0))
    else:
        body = functools.partial(_xattn_kernel, heads=X_HEADS)
        mem_spec = lambda g: pl.BlockSpec((None, mem_k.shape[1] // nb, d), lambda b, t: (g, b, 0))
    return pl.pallas_call(
        body, grid=(nb, tiles),
        in_specs=[qspec, mem_spec(k_index), mem_spec(v_index)],
        out_specs=qspec, out_shape=jax.ShapeDtypeStruct((n, d), q2.dtype),
        compiler_params=_params(("arbitrary", "arbitrary")), name="xattn",
    )(q2, mem_k, mem_v)


def _proj2_ffn_kernel(o_ref, x1_ref, wo_ref, g2_ref, b2_ref, wg_ref, wu_ref, wd_ref, g3_ref, b3_ref, out_ref,
                      *, alpha, ff_chunk):
    x2 = _layer_norm(alpha * x1_ref[...] + _dot(o_ref[...].astype(BF16), wo_ref[...]), g2_ref[...], b2_ref[...])
    xb = x2.astype(BF16)
    d_ff = wg_ref.shape[1]
    y = jnp.zeros(x2.shape, F32)
    for c in range(d_ff // ff_chunk):
        sl = slice(c * ff_chunk, (c + 1) * ff_chunk)
        gate = _dot(xb, wg_ref[:, sl])
        up = _dot(xb, wu_ref[:, sl])
        hid = gate * (1.0 / (1.0 + jnp.exp(-gate))) * up
        y = y + _dot(hid.astype(BF16), wd_ref[sl, :])
    out_ref[...] = _layer_norm(alpha * x2 + y, g3_ref[...], b3_ref[...])


def _proj2_ffn_call(o, x1, w_o, g2, b2, w_g, w_u, w_d, g3, b3, *, tm, alpha):
    n, d = x1.shape
    row = lambda i: (i, 0)
    consts = (w_o, g2, b2, w_g, w_u, w_d, g3, b3)
    return pl.pallas_call(
        functools.partial(_proj2_ffn_kernel, alpha=alpha, ff_chunk=256),
        grid=(n // tm,),
        in_specs=[pl.BlockSpec((tm, d), row), pl.BlockSpec((tm, d), row)] + [_resident(c.shape) for c in consts],
        out_specs=pl.BlockSpec((tm, d), row), out_shape=jax.ShapeDtypeStruct((n, d), F32),
        compiler_params=_params(("arbitrary",)), name="proj2_ffn",
    )(o, x1, *consts)


def _memkv_kernel(x_ref, w_ref, o_ref):
    o_ref[...] = _dot(x_ref[...].astype(BF16), w_ref[...])


def _memkv_call(mem, w_stack, *, tm):
    n, d = mem.shape
    g = w_stack.shape[0]
    return pl.pallas_call(
        _memkv_kernel, grid=(g, n // tm),
        in_specs=[pl.BlockSpec((tm, d), lambda j, i: (i, 0)), pl.BlockSpec((None, d, d), lambda j, i: (j, 0, 0))],
        out_specs=pl.BlockSpec((None, tm, d), lambda j, i: (j, i, 0)),
        out_shape=jax.ShapeDtypeStruct((g, n, d), F32),
        compiler_params=_params(("arbitrary", "arbitrary")), name="memkv",
    )(mem, w_stack)


def _block_diag(blocks):
    g, c, _ = blocks.shape
    eye = jnp.eye(g, dtype=blocks.dtype)
    return (eye[:, None, :, None] * blocks[:, :, None, :]).reshape(g * c, g * c)


def kernel(x_prompt, x_sample, mem_prompt, state_pool, cache_k, cache_v, cache_mem_k, cache_mem_v, page_table,
           w_in, pool_w, pool_scale, sg_norm_g, sg_norm_b, sg_w, sg_b, w_out, ln1_g, ln1_b, xq_w, xk_w, xv_w,
           xo_w, ln2_g, ln2_b, ffn_gate, ffn_up, ffn_down, ln3_g, ln3_b):
    n_layers = w_in.shape[0]
    bp, seq, d = x_prompt.shape
    bs, dec_seq, _ = x_sample.shape
    n_mem = mem_prompt.shape[1]
    n_phys = cache_k.shape[1]
    past_len = page_table.shape[1] * PAGE_SIZE
    assert past_len % MOBA_BLOCK == 0 and dec_seq <= MOBA_BLOCK and seq % ROW_TILE == 0
    alpha = (2 * n_layers) ** 0.25
    q_scale = (d // X_HEADS) ** -0.5

    xp = x_prompt.reshape(bp * seq, d)
    xs = x_sample.reshape(bs * dec_seq, d)
    paged = lambda c: jnp.transpose(c, (0, 1, 3, 4, 2)).reshape(n_layers * n_phys, MOBA_WIDTH, PAGE_SIZE)
    cache_k2, cache_v2 = paged(cache_k), paged(cache_v)
    hist16 = jnp.pad(state_pool, ((0, 0), (0, 0), (1, 0), (0, 0)))

    vec = lambda p: p.reshape(n_layers, 1, -1)
    w_in_b, w_out_b, xq_b, xo_b = (w.astype(BF16) for w in (w_in, w_out, xq_w, xo_w))
    wg_b, wu_b, wd_b = (w.astype(BF16) for w in (ffn_gate, ffn_up, ffn_down))
    pw_bd = jnp.stack([_block_diag(pool_w[l]) for l in range(n_layers)]).astype(BF16)
    causal = jnp.tril(jnp.ones((SG_CHUNK, SG_CHUNK), bool))
    sgw_p = jnp.where(causal, sg_w, 0.0).astype(BF16)
    sgb_p = jnp.repeat(jnp.swapaxes(sg_b, 1, 2), HEAD_DIM, axis=2)
    eye_s = jnp.eye(bs, dtype=F32)
    w_dec = jnp.where(causal[:dec_seq, :dec_seq], sg_w[:, :, :dec_seq, :dec_seq], 0.0)
    sgw_s = (eye_s[None, None, :, None, :, None] * w_dec[:, :, None, :, None, :]).reshape(
        n_layers, SG_HEADS, bs * dec_seq, bs * dec_seq).astype(BF16)
    sgb_s = jnp.tile(sgb_p[:, :dec_seq, :], (1, bs, 1))
    kv_stack = jnp.concatenate([xk_w, xv_w], axis=0).astype(BF16)

    memkv = _memkv_call(mem_prompt.reshape(bp * n_mem, d), kv_stack, tm=ROW_TILE)

    pool_p, pool_s, ks_l, vs_l, sgv_l = [], [], [], [], []
    kv_new = None
    for l in range(n_layers):
        ln = lambda p: vec(p)[l]
        yab, q, kt, vt, kb, hist = _in_prompt_call(
            xp, w_in_b[l], pw_bd[l], ln(pool_scale), ln(sg_norm_g), ln(sg_norm_b), sgw_p[l], sgb_p[l],
            kv_new, seq=seq, tm=IN_ROW_TILE, layer=l, n_layers=n_layers)
        kv_new = (kt, vt)
        ym = _moba_prompt_call(q, kb, vt, seq=seq, pairs_per_step=MOBA_PAIRS_PER_STEP, layer=l)
        x1, q2 = _proj1_call(yab, ym, xp, w_out_b[l], ln(ln1_g), ln(ln1_b), xq_b[l], tm=IN_ROW_TILE, alpha=alpha,
                             q_scale=q_scale, q_dtype=BF16)
        o = _xattn_call(q2, memkv, memkv, l, n_layers + l, rows_per_batch=seq, tm=ROW_TILE)
        xp = _proj2_ffn_call(o, x1, xo_b[l], ln(ln2_g), ln(ln2_b), wg_b[l], wu_b[l], wd_b[l], ln(ln3_g),
                             ln(ln3_b), tm=ROW_TILE, alpha=alpha)
        pool_p.append(hist[:, 1:, :])
        yab, q, k, v, hist, sgv = _in_sample_call(
            xs, hist16[l], w_in_b[l], pw_bd[l], ln(pool_scale), ln(sg_norm_g), ln(sg_norm_b), sgw_s[l], sgb_s[l],
            dec_seq=dec_seq, pos0=past_len)
        ym = _moba_decode_call(q, k, v, cache_k2, cache_v2, page_table, page_base=l * n_phys, dec_seq=dec_seq,
                               pps=DECODE_PAGES_PER_STEP)
        x1, q2 = _proj1_call(yab, ym, xs, w_out_b[l], ln(ln1_g), ln(ln1_b), xq_b[l], tm=bs * dec_seq, alpha=alpha,
                             q_scale=q_scale, q_dtype=F32)
        o = _xattn_call(q2, cache_mem_k, cache_mem_v, l, l, rows_per_batch=dec_seq, tm=dec_seq)
        xs = _proj2_ffn_call(o, x1, xo_b[l], ln(ln2_g), ln(ln2_b), wg_b[l], wu_b[l], wd_b[l], ln(ln3_g),
                             ln(ln3_b), tm=bs * dec_seq, alpha=alpha)
        pool_s.append(hist[:, 1:, :]); ks_l.append(k); vs_l.append(v); sgv_l.append(sgv)

    heads = lambda a, nb_, s_: a.reshape(nb_, s_, MOBA_HEADS, HEAD_DIM)
    tok_major = lambda a: jnp.transpose(a, (0, 1, 4, 2, 3))
    mem_shape = (n_layers, bp, n_mem, X_HEADS, d // X_HEADS)
    return (xp.reshape(bp, seq, d), xs.reshape(bs, dec_seq, d),
            jnp.stack(pool_p), jnp.stack(pool_s),
            tok_major(kv_new[0]), tok_major(kv_new[1]),
            jnp.stack([heads(a, bs, dec_seq) for a in ks_l]), jnp.stack([heads(a, bs, dec_seq) for a in vs_l]),
            jnp.stack([a.reshape(bs, dec_seq, SG_WIDTH) for a in sgv_l]),
            memkv[:n_layers].reshape(mem_shape), memkv[n_layers:].reshape(mem_shape))
```
